```python
import jax, jax.numpy as jnp
from jax import lax
import numpy as np

D_MODEL = 1024
BATCH = 8
SEQ = 4096
DEPTH = 1

D_HEAD = 64
N_HEADS_TOTAL = D_MODEL // D_HEAD
D_MIX = N_HEADS_TOTAL * D_HEAD
N_HEADS_DSA = N_HEADS_TOTAL // 2
N_HEADS_FOX = N_HEADS_TOTAL - N_HEADS_DSA
N_IDX_HEADS = 4
D_IDX = 64
TOP_K_MAX = 256
ROPE_THETA = 500000.0
ROT_DIM = D_HEAD // 4
ROT_DIM_IDX = D_IDX // 4
D_FF = ((8 * D_MODEL // 3 + 255) // 256) * 256
BLOCK_Q_FOX = 128
BLOCK_Q_DSA = 64
N_MOD = 6
RMS_EPS = 1e-6

SPLIT_SIZES = (
    N_HEADS_DSA * D_HEAD,
    N_HEADS_DSA * D_HEAD,
    N_HEADS_DSA * D_HEAD,
    N_IDX_HEADS * D_IDX,
    D_IDX,
    N_IDX_HEADS,
    N_HEADS_FOX * D_HEAD,
    N_HEADS_FOX * D_HEAD,
    N_HEADS_FOX * D_HEAD,
    N_HEADS_FOX,
)
D_IN = int(sum(SPLIT_SIZES))
SPLIT_POINTS = [int(v) for v in np.cumsum(SPLIT_SIZES)[:-1]]

kernel_name = "hymba_dsa_fox_adaln_block"


def rms_norm(x, g):
    xf = x.astype(jnp.float32)
    y = xf * lax.rsqrt(jnp.mean(xf * xf, axis=-1, keepdims=True) + RMS_EPS)
    return (y * g.astype(jnp.float32)).astype(x.dtype)


def modulate(h, shift, scale):
    return h * (1 + scale[:, None, :]) + shift[:, None, :]


def split_heads(a, n):
    return a.reshape(a.shape[0], a.shape[1], n, -1)


def rope_partial(x, positions, rot_dim):
    half = rot_dim // 2
    inv_freq = ROPE_THETA ** (-jnp.arange(0, rot_dim, 2, dtype=jnp.float32) / rot_dim)
    ang = positions.astype(jnp.float32)[..., None] * inv_freq
    cos = jnp.cos(ang)[:, :, None, :]
    sin = jnp.sin(ang)[:, :, None, :]
    xf = x.astype(jnp.float32)
    x1, x2, rest = xf[..., :half], xf[..., half:rot_dim], xf[..., rot_dim:]
    out = jnp.concatenate([x1 * cos - x2 * sin, x2 * cos + x1 * sin, rest], axis=-1)
    return out.astype(x.dtype)


def to_blocks(a, bq):
    b, s = a.shape[0], a.shape[1]
    return a.reshape(b, s // bq, bq, *a.shape[2:]).swapaxes(0, 1)


def from_blocks(a):
    a = a.swapaxes(0, 1)
    return a.reshape(a.shape[0], a.shape[1] * a.shape[2], *a.shape[3:])


def dsa_attention(q, k, v, iq, ik, iw, k_top):
    s_len = q.shape[1]
    scale = D_HEAD ** -0.5
    idx_scale = D_IDX ** -0.5
    key_pos = jnp.arange(s_len)
    ik32 = ik.astype(jnp.float32)

    def one_block(args):
        qb, iqb, iwb, qpos = args
        dots = jnp.einsum('bqhd,bsd->bqhs', iqb.astype(jnp.float32), ik32) * idx_scale
        score = jnp.einsum('bqhs,bqh->bqs', jax.nn.relu(dots), iwb.astype(jnp.float32))
        causal = key_pos[None, :] <= qpos[:, None]
        score = jnp.where(causal[None], score, -jnp.inf)
        _, idx = lax.top_k(score, k_top)
        valid = idx <= qpos[None, :, None]
        k_sel = jax.vmap(lambda kk, ii: kk[ii])(k, idx)
        v_sel = jax.vmap(lambda vv, ii: vv[ii])(v, idx)
        logits = jnp.einsum('bqhd,bqkhd->bhqk', qb, k_sel).astype(jnp.float32) * scale
        logits = jnp.where(valid[:, None], logits, -jnp.inf)
        p = jax.nn.softmax(logits, axis=-1).astype(v.dtype)
        return jnp.einsum('bhqk,bqkhd->bqhd', p, v_sel)

    qpos = jnp.arange(s_len).reshape(s_len // BLOCK_Q_DSA, BLOCK_Q_DSA)
    out = lax.map(one_block, (to_blocks(q, BLOCK_Q_DSA), to_blocks(iq, BLOCK_Q_DSA),
                              to_blocks(iw, BLOCK_Q_DSA), qpos))
    return from_blocks(out)


def fox_attention(q, k, v, log_f):
    s_len = q.shape[1]
    scale = D_HEAD ** -0.5
    cum = jnp.cumsum(log_f, axis=1)
    cum_keys = cum.transpose(0, 2, 1)
    key_pos = jnp.arange(s_len)

    def one_block(args):
        qb, cq, qpos = args
        logits = jnp.einsum('bqhd,bshd->bhqs', qb, k).astype(jnp.float32) * scale
        bias = cq.transpose(0, 2, 1)[..., :, None] - cum_keys[..., None, :]
        causal = key_pos[None, :] <= qpos[:, None]
        logits = jnp.where(causal[None, None], logits + bias, -jnp.inf)
        p = jax.nn.softmax(logits, axis=-1).astype(v.dtype)
        return jnp.einsum('bhqs,bshd->bqhd', p, v)

    qpos = jnp.arange(s_len).reshape(s_len // BLOCK_Q_FOX, BLOCK_Q_FOX)
    out = lax.map(one_block, (to_blocks(q, BLOCK_Q_FOX), to_blocks(cum, BLOCK_Q_FOX), qpos))
    return from_blocks(out)


def hybrid_mixer(h, positions, w_in, b_forget, dsa_norm_g, fox_norm_g, w_out, k_top):
    b, s, _ = h.shape
    proj = h @ w_in
    q_d, k_d, v_d, iq, ik, iw, q_f, k_f, v_f, f_logit = jnp.split(proj, SPLIT_POINTS, axis=-1)
    q_d = rope_partial(split_heads(q_d, N_HEADS_DSA), positions, ROT_DIM)
    k_d = rope_partial(split_heads(k_d, N_HEADS_DSA), positions, ROT_DIM)
    v_d = split_heads(v_d, N_HEADS_DSA)
    iq = rope_partial(split_heads(iq, N_IDX_HEADS), positions, ROT_DIM_IDX)
    ik = rope_partial(ik[:, :, None, :], positions, ROT_DIM_IDX)[:, :, 0, :]
    iw = iw * (N_IDX_HEADS ** -0.5)
    o_d = dsa_attention(q_d, k_d, v_d, iq, ik, iw, k_top).reshape(b, s, -1)
    log_f = jax.nn.log_sigmoid((f_logit + b_forget).astype(jnp.float32))
    o_f = fox_attention(split_heads(q_f, N_HEADS_FOX), split_heads(k_f, N_HEADS_FOX),
                        split_heads(v_f, N_HEADS_FOX), log_f).reshape(b, s, -1)
    y = jnp.concatenate([rms_norm(o_d, dsa_norm_g), rms_norm(o_f, fox_norm_g)], axis=-1)
    return y @ w_out


def swiglu(h, w_gate, w_up, w_down):
    return (jax.nn.silu(h @ w_gate) * (h @ w_up)) @ w_down


def setup_inputs(seed: int = 0) -> dict:
    key = jax.random.key(seed)
    ks = jax.random.split(key, 16)
    f32 = jnp.float32
    nrm = lambda k, shape, s: jax.random.normal(k, shape, f32) * s
    x = jax.random.normal(ks[0], (BATCH, SEQ, D_MODEL), f32)
    c = jax.random.normal(ks[1], (BATCH, D_MODEL), f32)
    offset = jax.random.randint(ks[2], (BATCH, 1), 0, 1024, dtype=jnp.int32)
    positions = (offset + jnp.arange(SEQ, dtype=jnp.int32)[None, :]).astype(jnp.int32)
    return {
        "x": x,
        "c": c,
        "positions": positions,
        "ada_w": nrm(ks[3], (DEPTH, D_MODEL, N_MOD * D_MODEL), 0.5 * D_MODEL ** -0.5),
        "ada_b": nrm(ks[4], (DEPTH, N_MOD * D_MODEL), 0.02),
        "norm_attn_g": 1.0 + nrm(ks[5], (DEPTH, D_MODEL), 0.02),
        "w_in": nrm(ks[6], (DEPTH, D_MODEL, D_IN), D_MODEL ** -0.5),
        "b_forget": 2.0 + nrm(ks[7], (DEPTH, N_HEADS_FOX), 0.5),
        "dsa_norm_g": 1.0 + nrm(ks[8], (DEPTH, N_HEADS_DSA * D_HEAD), 0.02),
        "fox_norm_g": 1.0 + nrm(ks[9], (DEPTH, N_HEADS_FOX * D_HEAD), 0.02),
        "w_out": nrm(ks[10], (DEPTH, D_MIX, D_MODEL), D_MIX ** -0.5),
        "norm_ffn_g": 1.0 + nrm(ks[11], (DEPTH, D_MODEL), 0.02),
        "w_gate": nrm(ks[12], (DEPTH, D_MODEL, D_FF), D_MODEL ** -0.5),
        "w_up": nrm(ks[13], (DEPTH, D_MODEL, D_FF), D_MODEL ** -0.5),
        "w_down": nrm(ks[14], (DEPTH, D_FF, D_MODEL), D_FF ** -0.5),
        "final_norm_g": 1.0 + nrm(ks[15], (D_MODEL,), 0.02),
    }


def reference(x, c, positions, ada_w, ada_b, norm_attn_g, w_in, b_forget, dsa_norm_g,
              fox_norm_g, w_out, norm_ffn_g, w_gate, w_up, w_down, final_norm_g):
    s_len = x.shape[1]
    k_top = min(TOP_K_MAX, s_len // 4)
    c_act = jax.nn.silu(c)
    for l in range(DEPTH):
        mod = c_act @ ada_w[l] + ada_b[l]
        sh1, sc1, g1, sh2, sc2, g2 = jnp.split(mod, N_MOD, axis=-1)
        h = modulate(rms_norm(x, norm_attn_g[l]), sh1, sc1)
        x = x + g1[:, None, :] * hybrid_mixer(h, positions, w_in[l], b_forget[l], dsa_norm_g[l],
                                              fox_norm_g[l], w_out[l], k_top)
        h = modulate(rms_norm(x, norm_ffn_g[l]), sh2, sc2)
        x = x + g2[:, None, :] * swiglu(h, w_gate[l], w_up[l], w_down[l])
    return rms_norm(x, final_norm_g)
```

```python
import functools

import numpy as np
import jax
import jax.numpy as jnp
from jax import lax
from jax.experimental import pallas as pl
from jax.experimental.pallas import tpu as pltpu

F32 = jnp.float32
BF16 = jnp.bfloat16

D_HEAD = 64
N_IDX_HEADS = 4
D_IDX = 64
TOP_K_MAX = 256
ROPE_THETA = 500000.0
ROT_DIM = D_HEAD // 4
N_MOD = 6
RMS_EPS = 1e-6

LANES = 128
VMEM_LIMIT = 56 * 1024 * 1024

INT_MIN = -(2 ** 31)
KEY_NEG_INF = int(np.int32(np.uint32(0xFF800000) ^ np.uint32(0x7FFFFFFF)))

NT_DIMS = (((1,), (1,)), ((), ()))


def _cparams(sem):
    return pltpu.CompilerParams(dimension_semantics=sem, vmem_limit_bytes=VMEM_LIMIT)


def _split3(a):
    a0 = a.astype(BF16)
    r = a - a0.astype(F32)
    a1 = r.astype(BF16)
    a2 = (r - a1.astype(F32)).astype(BF16)
    return a0, a1, a2


def _rms(x, g):
    return x * lax.rsqrt(jnp.mean(x * x, axis=-1, keepdims=True) + RMS_EPS) * g


def _mod_kernel(c_ref, w_ref, b_ref, o_ref):
    c = c_ref[...]
    ca = c * jax.nn.sigmoid(c)
    a0, a1, _ = _split3(ca)
    w = w_ref[...]
    w0 = w.astype(BF16)
    w1 = (w - w0.astype(F32)).astype(BF16)
    dot = functools.partial(jnp.dot, preferred_element_type=F32)
    o_ref[...] = dot(a0, w0) + (dot(a0, w1) + dot(a1, w0)) + b_ref[...]


def _modulation(c, w, b):
    bsz, d = c.shape
    n = w.shape[1]
    tn = 1024
    return pl.pallas_call(
        _mod_kernel,
        grid=(n // tn,),
        in_specs=[pl.BlockSpec((bsz, d), lambda j: (0, 0)),
                  pl.BlockSpec((d, tn), lambda j: (0, j)),
                  pl.BlockSpec((1, tn), lambda j: (0, j))],
        out_specs=pl.BlockSpec((bsz, tn), lambda j: (0, j)),
        out_shape=jax.ShapeDtypeStruct((bsz, n), F32),
        compiler_params=_cparams(("arbitrary",)),
        name="adaln_mod",
    )(c, w, b.reshape(1, n))


def _rope(x, cosv, sin_signed, first_half):
    partner = jnp.where(first_half, pltpu.roll(x, LANES - 8, 1), pltpu.roll(x, 8, 1))
    return x * cosv + partner * sin_signed


def _inproj_kernel(x_ref, sh_ref, sc_ref, g_ref, pos_ref, invf_ref, w_ref, bf_ref,
                   qd_ref, kd_ref, vd_ref, qf_ref, kf_ref, vf_ref, iq_ref, ik_ref, aux_ref, cumt_ref,
                   carry_ref, *, hd, tm, tc):
    i = pl.program_id(1)
    h = _rms(x_ref[0], g_ref[...]) * (1.0 + sc_ref[0]) + sh_ref[0]
    proj = jnp.dot(h.astype(BF16), w_ref[...], preferred_element_type=F32)

    lane = lax.broadcasted_iota(jnp.int32, (tm, LANES), 1)
    ang = pos_ref[0].astype(F32) * invf_ref[...]
    cosv = jnp.cos(ang)
    sinv = jnp.sin(ang)
    first_half = (lane & (D_HEAD - 1)) < (ROT_DIM // 2)
    sin_signed = jnp.where(first_half, -sinv, sinv)
    rope = functools.partial(_rope, cosv=cosv, sin_signed=sin_signed, first_half=first_half)

    def slab(base, j):
        return proj[:, base + j * LANES: base + (j + 1) * LANES]

    scale = D_HEAD ** -0.5
    idx_scale = D_IDX ** -0.5
    for j in range(hd // LANES):
        qd_ref[0, :, j * LANES:(j + 1) * LANES] = (rope(slab(0, j)) * scale).astype(BF16)
        kd_ref[0, :, j * LANES:(j + 1) * LANES] = rope(slab(hd, j)).astype(BF16)
        vd_ref[0, :, j * LANES:(j + 1) * LANES] = slab(2 * hd, j).astype(BF16)
        qf_ref[0, :, j * LANES:(j + 1) * LANES] = (slab(3 * hd, j) * scale).astype(BF16)
        kf_ref[0, :, j * LANES:(j + 1) * LANES] = slab(4 * hd, j).astype(BF16)
        vf_ref[0, :, j * LANES:(j + 1) * LANES] = slab(5 * hd, j).astype(BF16)
    base = 6 * hd
    for j in range(N_IDX_HEADS * D_IDX // LANES):
        iq_ref[0, :, j * LANES:(j + 1) * LANES] = (rope(slab(base, j)) * idx_scale).astype(BF16)
    base += N_IDX_HEADS * D_IDX
    ik_ref[0] = rope(slab(base, 0)).astype(BF16)
    base += LANES

    small = slab(base, 0)
    z = small + bf_ref[...]
    logf = jnp.minimum(z, 0.0) - jnp.log1p(jnp.exp(-jnp.abs(z)))
    is_f = (lane >= 8) & (lane < 16)
    logf = jnp.where(is_f, logf, 0.0)

    @pl.when(i == 0)
    def _():
        carry_ref[...] = jnp.zeros_like(carry_ref)

    r = lax.broadcasted_iota(jnp.int32, (tm, tm), 0)
    c = lax.broadcasted_iota(jnp.int32, (tm, tm), 1)
    tri = jnp.where(c <= r, 1.0, 0.0).astype(BF16)
    l0, l1, l2 = _split3(logf)
    dot = functools.partial(jnp.dot, preferred_element_type=F32)
    cum = dot(tri, l0) + (dot(tri, l1) + dot(tri, l2)) + carry_ref[...]
    carry_ref[...] = cum[tm - 1:tm, :]
    aux = jnp.where(lane < N_IDX_HEADS, small * (N_IDX_HEADS ** -0.5), cum)
    aux_ref[0] = aux
    for u in range(tm // tc):
        cumt_ref[0, u] = aux[u * tc:(u + 1) * tc, :].T[8:16, :]


def _inproj(x, sh, sc, g, pos, invf, w, bfv, *, hd, tm, tc):
    bsz, s, d = x.shape
    n_cols = w.shape[1]
    nt = s // tm
    tok = lambda width: pl.BlockSpec((1, tm, width), lambda b, i: (b, i, 0))
    per_b = pl.BlockSpec((1, 1, d), lambda b, i: (b, 0, 0))
    const = lambda shape: pl.BlockSpec(shape, lambda b, i: (0,) * len(shape))
    bf = lambda width: jax.ShapeDtypeStruct((bsz, s, width), BF16)
    out_shape = [bf(hd)] * 6 + [bf(N_IDX_HEADS * D_IDX), bf(LANES),
                                jax.ShapeDtypeStruct((bsz, s, LANES), F32),
                                jax.ShapeDtypeStruct((bsz, s // tc, 8, tc), F32)]
    out_specs = [tok(hd)] * 6 + [tok(N_IDX_HEADS * D_IDX), tok(LANES), tok(LANES),
                                 pl.BlockSpec((1, tm // tc, 8, tc), lambda b, i: (b, i, 0, 0))]
    return pl.pallas_call(
        functools.partial(_inproj_kernel, hd=hd, tm=tm, tc=tc),
        grid=(bsz, nt),
        in_specs=[tok(d), per_b, per_b, const((1, d)), tok(1), const((1, LANES)),
                  const((d, n_cols)), const((1, LANES))],
        out_specs=out_specs,
        out_shape=out_shape,
        scratch_shapes=[pltpu.VMEM((1, LANES), F32)],
        compiler_params=_cparams(("arbitrary", "arbitrary")),
        name="inproj",
    )(x, sh, sc, g, pos, invf, w, bfv)


def _head_masks(t):
    lane = lax.broadcasted_iota(jnp.int32, (t, LANES), 1)
    lo = lane < D_HEAD
    return lo, (jnp.where(lo, 1.0, 0.0).astype(BF16), jnp.where(lo, 0.0, 1.0).astype(BF16))


def _flash_head(qm, k_ref, v_ref, j, n_kb, t, bias_fn):
    def body(kb, carry):
        m, l, acc = carry
        off = pl.multiple_of(kb * t, t)
        k2 = k_ref[0, pl.ds(off, t), j * LANES:(j + 1) * LANES]
        v2 = v_ref[0, pl.ds(off, t), j * LANES:(j + 1) * LANES]
        s = lax.dot_general(qm, k2, NT_DIMS, preferred_element_type=F32) + bias_fn(kb)
        m_new = jnp.maximum(m, jnp.max(s, axis=1, keepdims=True))
        alpha = jnp.exp(m - m_new)
        p = jnp.exp(s - m_new)
        l = alpha * l + jnp.sum(p, axis=1, keepdims=True)
        acc = alpha * acc + jnp.dot(p.astype(BF16), v2, preferred_element_type=F32)
        return m_new, l, acc

    m0 = jnp.full((t, 1), -1e30, F32)
    l0 = jnp.zeros((t, 1), F32)
    acc0 = jnp.zeros((t, LANES), F32)
    _, l, acc = lax.fori_loop(0, n_kb, body, (m0, l0, acc0))
    return acc / l


def _attend_all_heads(q_ref, k_ref, v_ref, g_ref, o_ref, osc_ref, n_kb, t, n_heads, bias_fn):
    lo, hmask = _head_masks(t)
    for j in range(n_heads // 2):
        q2 = q_ref[0, :, j * LANES:(j + 1) * LANES]
        o0 = _flash_head(q2 * hmask[0], k_ref, v_ref, j, n_kb, t, functools.partial(bias_fn, 2 * j))
        o1 = _flash_head(q2 * hmask[1], k_ref, v_ref, j, n_kb, t, functools.partial(bias_fn, 2 * j + 1))
        osc_ref[:, j * LANES:(j + 1) * LANES] = jnp.where(lo, o0, o1)
    o_ref[0] = _rms(osc_ref[...], g_ref[...]).astype(o_ref.dtype)


def _fox_kernel(q_ref, k_ref, v_ref, aux_ref, cumt_ref, g_ref, o_ref, osc_ref, *, t, n_heads):
    qi = pl.program_id(1)
    row = qi * t + lax.broadcasted_iota(jnp.int32, (t, t), 0)
    col = lax.broadcasted_iota(jnp.int32, (t, t), 1)
    aux = aux_ref[0]

    def bias_fn(h, kb):
        cq = aux[:, 8 + h:9 + h]
        ck = cumt_ref[0, kb, h:h + 1, :]
        return jnp.where(col + kb * t <= row, cq - ck, -jnp.inf)

    _attend_all_heads(q_ref, k_ref, v_ref, g_ref, o_ref, osc_ref, qi + 1, t, n_heads, bias_fn)


def _fox(q, k, v, aux, cumt, g, *, t):
    bsz, s, hd = q.shape
    n_heads = hd // D_HEAD
    nt = s // t
    qblk = lambda width: pl.BlockSpec((1, t, width), lambda b, i: (b, i, 0))
    full = pl.BlockSpec((1, s, hd), lambda b, i: (b, 0, 0))
    return pl.pallas_call(
        functools.partial(_fox_kernel, t=t, n_heads=n_heads),
        grid=(bsz, nt),
        in_specs=[qblk(hd), full, full, qblk(LANES),
                  pl.BlockSpec((1, nt, 8, t), lambda b, i: (b, 0, 0, 0)),
                  pl.BlockSpec((1, hd), lambda b, i: (0, 0))],
        out_specs=qblk(hd),
        out_shape=jax.ShapeDtypeStruct((bsz, s, hd), BF16),
        scratch_shapes=[pltpu.VMEM((t, hd), F32)],
        compiler_params=_cparams(("arbitrary", "arbitrary")),
        name="fox_attn",
    )(q, k, v, aux, cumt, g)


def _dsa_kernel(q_ref, iq_ref, aux_ref, k_ref, v_ref, ik_ref, g_ref, o_ref,
                key_ref, bias_ref, thr_ref, osc_ref, *, t, n_heads, k_top):
    qi = pl.program_id(1)
    n_kb = qi + 1
    row = qi * t + lax.broadcasted_iota(jnp.int32, (t, t), 0)
    col = lax.broadcasted_iota(jnp.int32, (t, t), 1)
    _, hmask = _head_masks(t)
    aux = aux_ref[0]

    iqm = []
    for h in range(N_IDX_HEADS):
        iq2 = iq_ref[0, :, (h // 2) * LANES:(h // 2 + 1) * LANES]
        iqm.append(iq2 * hmask[h % 2])

    def score_body(kb, _):
        off = pl.multiple_of(kb * t, t)
        ik2 = ik_ref[0, pl.ds(off, t), :]
        sc = jnp.zeros((t, t), F32)
        for h in range(N_IDX_HEADS):
            d = lax.dot_general(iqm[h], ik2, NT_DIMS, preferred_element_type=F32)
            sc = sc + jnp.maximum(d, 0.0) * aux[:, h:h + 1]
        sc = jnp.where(col + kb * t <= row, sc, -jnp.inf)
        bits = pltpu.bitcast(sc, jnp.int32)
        key = bits ^ ((bits >> 31) & 0x7FFFFFFF)
        key_ref[kb] = jnp.where(key == -1, 0, key)
        return 0

    lax.fori_loop(0, n_kb, score_body, 0)

    def count(cmp_fn):
        def body(kb, part):
            kk = key_ref[kb]
            for c in range(t // LANES):
                part = part + jnp.where(cmp_fn(kk[:, c * LANES:(c + 1) * LANES]), 1.0, 0.0)
            return part
        part = lax.fori_loop(0, n_kb, body, jnp.zeros((t, LANES), F32))
        return jnp.sum(part, axis=1, keepdims=True)

    thr_ref[...] = jnp.full(thr_ref.shape, KEY_NEG_INF + 1, jnp.int32)

    @pl.when(n_kb * t > k_top)
    def _():
        def search(it, u):
            bit = lax.shift_left(jnp.int32(1), 31 - it)
            cand = (u | bit) ^ INT_MIN
            cnt = count(lambda kk: kk >= cand)
            return jnp.where(cnt >= k_top, u | bit, u)
        u = lax.fori_loop(0, 32, search, jnp.zeros((t, 1), jnp.int32))
        thr_ref[...] = jnp.broadcast_to(jnp.maximum(u ^ INT_MIN, KEY_NEG_INF + 1), thr_ref.shape)

    thr = thr_ref[:, 0:1]
    n_ge = count(lambda kk: kk >= thr)
    has_ties = jnp.max(n_ge) > k_top

    @pl.when(jnp.logical_not(has_ties))
    def _():
        def body(kb, _):
            bias_ref[kb] = jnp.where(key_ref[kb] >= thr, 0.0, -jnp.inf)
            return 0
        lax.fori_loop(0, n_kb, body, 0)

    @pl.when(has_ties)
    def _():
        need = k_top - count(lambda kk: kk > thr)
        r = lax.broadcasted_iota(jnp.int32, (t, t), 0)
        c = lax.broadcasted_iota(jnp.int32, (t, t), 1)
        before = jnp.where(r < c, 1.0, 0.0).astype(BF16)

        def body(kb, run):
            kk = key_ref[kb]
            eq = jnp.where(kk == thr, 1.0, 0.0)
            rank = jnp.dot(eq.astype(BF16), before, preferred_element_type=F32) + run
            take = jnp.where(rank < need, eq, 0.0)
            sel = jnp.where(kk > thr, 1.0, take)
            bias_ref[kb] = jnp.where(sel > 0.5, 0.0, -jnp.inf)
            return run + jnp.sum(eq, axis=1, keepdims=True)
        lax.fori_loop(0, n_kb, body, jnp.zeros((t, 1), F32))

    _attend_all_heads(q_ref, k_ref, v_ref, g_ref, o_ref, osc_ref, n_kb, t, n_heads,
                      lambda h, kb: bias_ref[kb])


def _dsa(q, iq, aux, k, v, ik, g, *, t, k_top):
    bsz, s, hd = q.shape
    n_heads = hd // D_HEAD
    nt = s // t
    qblk = lambda width: pl.BlockSpec((1, t, width), lambda b, i: (b, i, 0))
    full = lambda width: pl.BlockSpec((1, s, width), lambda b, i: (b, 0, 0))
    return pl.pallas_call(
        functools.partial(_dsa_kernel, t=t, n_heads=n_heads, k_top=k_top),
        grid=(bsz, nt),
        in_specs=[qblk(hd), qblk(N_IDX_HEADS * D_IDX), qblk(LANES), full(hd), full(hd), full(LANES),
                  pl.BlockSpec((1, hd), lambda b, i: (0, 0))],
        out_specs=qblk(hd),
        out_shape=jax.ShapeDtypeStruct((bsz, s, hd), BF16),
        scratch_shapes=[pltpu.VMEM((nt, t, t), jnp.int32), pltpu.VMEM((nt, t, t), F32),
                        pltpu.VMEM((t, LANES), jnp.int32), pltpu.VMEM((t, hd), F32)],
        compiler_params=_cparams(("arbitrary", "arbitrary")),
        name="dsa_attn",
    )(q, iq, aux, k, v, ik, g)


def _ffn_kernel(x_ref, yd_ref, yf_ref, g1_ref, sh_ref, sc_ref, g2_ref, ng_ref, fg_ref,
                wo_ref, wg_ref, wu_ref, wd_ref, o_ref, x1_ref, h_ref, acc_ref, *, hd):
    c = pl.program_id(2)
    dot = functools.partial(jnp.dot, preferred_element_type=F32)

    @pl.when(c == 0)
    def _():
        mix = dot(yd_ref[0], wo_ref[0:hd, :]) + dot(yf_ref[0], wo_ref[hd:, :])
        x1 = x_ref[0] + g1_ref[0] * mix
        x1_ref[...] = x1
        h_ref[...] = (_rms(x1, ng_ref[...]) * (1.0 + sc_ref[0]) + sh_ref[0]).astype(BF16)
        acc_ref[...] = jnp.zeros_like(acc_ref)

    h = h_ref[...]
    gate = dot(h, wg_ref[0])
    up = dot(h, wu_ref[0])
    act = gate * jax.nn.sigmoid(gate) * up
    acc_ref[...] += dot(act.astype(BF16), wd_ref[0])

    @pl.when(c == pl.num_programs(2) - 1)
    def _():
        x2 = x1_ref[...] + g2_ref[0] * acc_ref[...]
        o_ref[0] = _rms(x2, fg_ref[...])


def _ffn(x, yd, yf, g1, sh, sc, g2, ng, fg, wo, wg, wu, wd, *, tm):
    bsz, s, d = x.shape
    hd = yd.shape[-1]
    n_chunks, _, fc = wg.shape
    tok = lambda width: pl.BlockSpec((1, tm, width), lambda b, i, c: (b, i, 0))
    per_b = pl.BlockSpec((1, 1, d), lambda b, i, c: (b, 0, 0))
    vec = pl.BlockSpec((1, d), lambda b, i, c: (0, 0))
    return pl.pallas_call(
        functools.partial(_ffn_kernel, hd=hd),
        grid=(bsz, s // tm, n_chunks),
        in_specs=[tok(d), tok(hd), tok(hd), per_b, per_b, per_b, per_b, vec, vec,
                  pl.BlockSpec((2 * hd, d), lambda b, i, c: (0, 0)),
                  pl.BlockSpec((1, d, fc), lambda b, i, c: (c, 0, 0)),
                  pl.BlockSpec((1, d, fc), lambda b, i, c: (c, 0, 0)),
                  pl.BlockSpec((1, fc, d), lambda b, i, c: (c, 0, 0))],
        out_specs=tok(d),
        out_shape=jax.ShapeDtypeStruct((bsz, s, d), F32),
        scratch_shapes=[pltpu.VMEM((tm, d), F32), pltpu.VMEM((tm, d), BF16), pltpu.VMEM((tm, d), F32)],
        compiler_params=_cparams(("arbitrary", "arbitrary", "arbitrary")),
        name="outproj_ffn",
    )(x, yd, yf, g1, sh, sc, g2, ng, fg, wo, wg, wu, wd)


def _pick_tile(n, target):
    t = min(n, target)
    while n % t:
        t //= 2
    return t


def _ff_chunk(d_ff):
    units = d_ff // LANES
    best = 1
    for u in range(1, units + 1):
        if units % u == 0 and u * LANES <= 1536:
            best = u
    return best * LANES


def kernel(x, c, positions, ada_w, ada_b, norm_attn_g, w_in, b_forget, dsa_norm_g, fox_norm_g,
           w_out, norm_ffn_g, w_gate, w_up, w_down, final_norm_g):
    bsz, s, d = x.shape
    depth = ada_w.shape[0]
    n_fox = b_forget.shape[1]
    hd = dsa_norm_g.shape[1]
    n_dsa = hd // D_HEAD
    assert n_fox == n_dsa == 8 and hd % LANES == 0
    k_top = min(TOP_K_MAX, s // 4)
    t_attn = _pick_tile(s, 256)
    tm_in = _pick_tile(s, 512)
    tm_ffn = _pick_tile(s, 512)
    assert k_top % t_attn == 0 or t_attn % k_top == 0

    sizes = (hd, hd, hd, N_IDX_HEADS * D_IDX, D_IDX, N_IDX_HEADS, hd, hd, hd, n_fox)
    pts = np.cumsum((0,) + sizes)
    seg = lambda wl, k: wl[:, pts[k]:pts[k + 1]]

    lane = jnp.arange(LANES)
    inv_freq = ROPE_THETA ** (-jnp.arange(0, ROT_DIM, 2, dtype=F32) / ROT_DIM)
    invf = jnp.where((lane % D_HEAD) < ROT_DIM, inv_freq[lane % (ROT_DIM // 2)], 0.0).reshape(1, LANES)
    pos = positions.reshape(bsz, s, 1)

    assert depth == 1, "the fused final RMSNorm assumes a single layer"
    mod = _modulation(c, ada_w[0], ada_b[0])
    sh1, sc1, g1, sh2, sc2, g2 = [m.reshape(bsz, 1, d) for m in jnp.split(mod, N_MOD, axis=-1)]

    wl = w_in[0]
    small = jnp.zeros((d, LANES), F32)
    small = small.at[:, 0:N_IDX_HEADS].set(seg(wl, 5)).at[:, 8:8 + n_fox].set(seg(wl, 9))
    w_cat = jnp.concatenate([seg(wl, 0), seg(wl, 1), seg(wl, 2), seg(wl, 6), seg(wl, 7), seg(wl, 8),
                             seg(wl, 3), seg(wl, 4), seg(wl, 4), small], axis=1).astype(BF16)
    bfv = jnp.zeros((1, LANES), F32).at[0, 8:8 + n_fox].set(b_forget[0])

    qd, kd, vd, qf, kf, vf, iq, ik, aux, cumt = _inproj(
        x, sh1, sc1, norm_attn_g[0].reshape(1, d), pos, invf, w_cat, bfv, hd=hd, tm=tm_in, tc=t_attn)

    yf = _fox(qf, kf, vf, aux, cumt, fox_norm_g[0].reshape(1, hd), t=t_attn)
    yd = _dsa(qd, iq, aux, kd, vd, ik, dsa_norm_g[0].reshape(1, hd), t=t_attn, k_top=k_top)

    d_ff = w_gate.shape[2]
    fc = _ff_chunk(d_ff)
    n_chunks = d_ff // fc
    wg = w_gate[0].astype(BF16).reshape(d, n_chunks, fc).transpose(1, 0, 2)
    wu = w_up[0].astype(BF16).reshape(d, n_chunks, fc).transpose(1, 0, 2)
    wd = w_down[0].astype(BF16).reshape(n_chunks, fc, d)
    return _ffn(x, yd, yf, g1, sh2, sc2, g2, norm_ffn_g[0].reshape(1, d), final_norm_g.reshape(1, d),
                w_out[0].astype(BF16), wg, wu, wd, tm=tm_ffn)
```

```python
import functools

import numpy as np
import jax
import jax.numpy as jnp
from jax import lax
from jax.experimental import pallas as pl
from jax.experimental.pallas import tpu as pltpu

F32 = jnp.float32
BF16 = jnp.bfloat16

D_HEAD = 64
N_IDX_HEADS = 4
D_IDX = 64
TOP_K_MAX = 256
ROPE_THETA = 500000.0
ROT_DIM = D_HEAD // 4
N_MOD = 6
RMS_EPS = 1e-6

LOG2E = 1.4426950408889634
V_ROWS = 80
LANES = 128
SUBLANES = 8
VMEM_LIMIT = 56 * 1024 * 1024

INT_MIN = -(2 ** 31)
NEG_INF_BITS = int(np.int32(np.uint32(0xFF800000)))
KEY_NEG_INF = int(np.int32(np.uint32(0xFF800000) ^ np.uint32(0x7FFFFFFF)))

NT_DIMS = (((1,), (1,)), ((), ()))


def _cparams(sem):
    return pltpu.CompilerParams(dimension_semantics=sem, vmem_limit_bytes=VMEM_LIMIT)


def _split3(a):
    a0 = a.astype(BF16).astype(F32)
    a1 = (a - a0).astype(BF16).astype(F32)
    return a0, a1, a - a0 - a1


def _rms(x, g):
    return x * lax.rsqrt(jnp.mean(x * x, axis=-1, keepdims=True) + RMS_EPS) * g


def _mod_kernel(c_ref, w_ref, b_ref, o_ref):
    c = c_ref[...]
    a0, a1, _ = _split3(c * jax.nn.sigmoid(c))
    w0, w1, _ = _split3(w_ref[...])
    dot = lambda a, b: jnp.dot(a.astype(BF16), b.astype(BF16), preferred_element_type=F32)
    o_ref[...] = dot(a0, w0) + (dot(a0, w1) + dot(a1, w0)) + b_ref[...]


def _modulation(c, w, b):
    bsz, d = c.shape
    n = w.shape[1]
    tn = 1024
    return pl.pallas_call(
        _mod_kernel,
        grid=(n // tn,),
        in_specs=[pl.BlockSpec((bsz, d), lambda j: (0, 0)),
                  pl.BlockSpec((d, tn), lambda j: (0, j)),
                  pl.BlockSpec((1, tn), lambda j: (0, j))],
        out_specs=pl.BlockSpec((bsz, tn), lambda j: (0, j)),
        out_shape=jax.ShapeDtypeStruct((bsz, n), F32),
        compiler_params=_cparams(("arbitrary",)),
        name="adaln_mod",
    )(c, w, b.reshape(1, n))


def _inproj_kernel(x_ref, sh_ref, sc_ref, g_ref, pos_ref, invf_ref, w_ref, bf_ref,
                   qxd_ref, kxd_ref, vtd_ref, qxf_ref, kxf_ref, vtf_ref, iq_ref, ik_ref, auxt_ref,
                   carry_ref, *, hd, tm, tk):
    i = pl.program_id(1)
    h = _rms(x_ref[0], g_ref[...]) * (1.0 + sc_ref[0]) + sh_ref[0]
    proj = jnp.dot(h.astype(BF16), w_ref[...], preferred_element_type=F32)

    lane = lax.broadcasted_iota(jnp.int32, (tm, LANES), 1)
    ang = pos_ref[0].astype(F32) * invf_ref[...]
    cosv = jnp.cos(ang)
    sinv = jnp.sin(ang)
    first_half = (lane & (D_HEAD - 1)) < (ROT_DIM // 2)
    sin_signed = jnp.where(first_half, -sinv, sinv)

    def rope(x):
        partner = jnp.where(first_half, pltpu.roll(x, LANES - ROT_DIM // 2, 1), pltpu.roll(x, ROT_DIM // 2, 1))
        return x * cosv + partner * sin_signed

    def slab(base, j):
        return proj[:, base + j * LANES: base + (j + 1) * LANES]

    def head_of(x, h):
        return pltpu.roll(x, D_HEAD, 1) if h % 2 else x

    def put_vt(ref, h, v_slab):
        one_row = jnp.where(lane == D_HEAD, 1.0, 0.0)
        vt = jnp.where(lane < D_HEAD, head_of(v_slab, h), one_row).T[0:V_ROWS, :]
        for u in range(tm // tk):
            ref[0, h, u] = vt[:, u * tk:(u + 1) * tk].astype(BF16)

    qscale = (D_HEAD ** -0.5) * LOG2E
    in_head = lane < D_HEAD

    for j in range(hd // LANES):
        rq = rope(slab(0, j)) * qscale
        rk = rope(slab(hd, j))
        for h in (2 * j, 2 * j + 1):
            qxd_ref[0, h] = jnp.where(in_head, head_of(rq, h), 0.0).astype(BF16)
            kxd_ref[0, h] = jnp.where(in_head, head_of(rk, h), 0.0).astype(BF16)
            put_vt(vtd_ref, h, slab(2 * hd, j))
    base = 6 * hd
    for j in range(N_IDX_HEADS * D_IDX // LANES):
        iq_ref[0, :, j * LANES:(j + 1) * LANES] = (rope(slab(base, j)) * (D_IDX ** -0.5)).astype(BF16)
    base += N_IDX_HEADS * D_IDX
    ik_ref[0] = rope(slab(base, 0)).astype(BF16)
    base += LANES

    small = slab(base, 0)
    z = small + bf_ref[...]
    logf = jnp.minimum(z, 0.0) - jnp.log1p(jnp.exp(-jnp.abs(z)))
    logf = jnp.where((lane >= 8) & (lane < 16), logf, 0.0)

    @pl.when(i == 0)
    def _():
        carry_ref[...] = jnp.zeros_like(carry_ref)

    r = lax.broadcasted_iota(jnp.int32, (tm, tm), 0)
    c = lax.broadcasted_iota(jnp.int32, (tm, tm), 1)
    tri = jnp.where(c <= r, 1.0, 0.0).astype(BF16)
    dot = lambda a, b: jnp.dot(a, b.astype(BF16), preferred_element_type=F32)
    l0, l1, l2 = _split3(logf)
    cum = dot(tri, l0) + (dot(tri, l1) + dot(tri, l2)) + carry_ref[...]
    carry_ref[...] = cum[tm - 1:tm, :]

    aux = jnp.where(lane < N_IDX_HEADS, small * (N_IDX_HEADS ** -0.5), 0.0)
    for u in range(tm // tk):
        auxt_ref[0, u] = aux[u * tk:(u + 1) * tk, :].T[0:SUBLANES, :]

    ncum = cum * (-LOG2E)
    ones3 = jnp.where(lane < D_HEAD + 3, 1.0, 0.0)
    for j in range(hd // LANES):
        for h in (2 * j, 2 * j + 1):
            qxf_ref[0, h] = jnp.where(in_head, head_of(slab(3 * hd, j), h) * qscale, ones3).astype(BF16)
            p0, p1, p2 = _split3(ncum[:, 8 + h:9 + h])
            extra = jnp.where(lane == D_HEAD, p0,
                              jnp.where(lane == D_HEAD + 1, p1, jnp.where(lane == D_HEAD + 2, p2, 0.0)))
            kxf_ref[0, h] = jnp.where(in_head, head_of(slab(4 * hd, j), h), extra).astype(BF16)
            put_vt(vtf_ref, h, slab(5 * hd, j))


def _inproj(x, sh, sc, g, pos, invf, w, bfv, *, hd, tm, tk):
    bsz, s, d = x.shape
    n_cols = w.shape[1]
    nh = hd // D_HEAD
    tok = lambda width: pl.BlockSpec((1, tm, width), lambda b, i: (b, i, 0))
    per_b = pl.BlockSpec((1, 1, d), lambda b, i: (b, 0, 0))
    const = lambda shape: pl.BlockSpec(shape, lambda b, i: (0,) * len(shape))
    headed = jax.ShapeDtypeStruct((bsz, nh, s, LANES), BF16)
    headed_spec = pl.BlockSpec((1, nh, tm, LANES), lambda b, i: (b, 0, i, 0))
    vt = jax.ShapeDtypeStruct((bsz, nh, s // tk, V_ROWS, tk), BF16)
    vt_spec = pl.BlockSpec((1, nh, tm // tk, V_ROWS, tk), lambda b, i: (b, 0, i, 0, 0))
    out_shape = [headed, headed, vt, headed, headed, vt,
                 jax.ShapeDtypeStruct((bsz, s, N_IDX_HEADS * D_IDX), BF16),
                 jax.ShapeDtypeStruct((bsz, s, LANES), BF16),
                 jax.ShapeDtypeStruct((bsz, s // tk, SUBLANES, tk), F32)]
    out_specs = [headed_spec, headed_spec, vt_spec, headed_spec, headed_spec, vt_spec,
                 tok(N_IDX_HEADS * D_IDX), tok(LANES),
                 pl.BlockSpec((1, tm // tk, SUBLANES, tk), lambda b, i: (b, i, 0, 0))]
    return pl.pallas_call(
        functools.partial(_inproj_kernel, hd=hd, tm=tm, tk=tk),
        grid=(bsz, s // tm),
        in_specs=[tok(d), per_b, per_b, const((1, d)), tok(1), const((1, LANES)),
                  const((d, n_cols)), const((1, LANES))],
        out_specs=out_specs,
        out_shape=out_shape,
        scratch_shapes=[pltpu.VMEM((1, LANES), F32)],
        compiler_params=_cparams(("arbitrary", "arbitrary")),
        name="inproj",
    )(x, sh, sc, g, pos, invf, w, bfv)


def _flash_t(q, kx_ref, vt_ref, heads, n_plain, t, plain_fn, last_fn):
    def step(kb, carry, fn):
        off = pl.multiple_of(kb * t, t)
        new = []
        for e, h in enumerate(heads):
            m, acc = carry[e]
            st = fn(lax.dot_general(kx_ref[0, h, pl.ds(off, t), :], q[e], NT_DIMS,
                                    preferred_element_type=F32), kb)
            m_new = jnp.maximum(m, jnp.max(st, axis=0, keepdims=True))
            pt = jnp.exp2(st - m_new).astype(BF16)
            acc = jnp.exp2(m - m_new) * acc + jnp.dot(vt_ref[0, h, kb], pt, preferred_element_type=F32)
            new.append((m_new, acc))
        return tuple(new)

    init = tuple((jnp.full((1, t), -1e30, F32), jnp.zeros((V_ROWS, t), F32)) for _ in heads)
    carry = lax.fori_loop(0, n_plain, lambda kb, c: step(kb, c, plain_fn), init)
    carry = step(n_plain, carry, last_fn)
    o = jnp.concatenate([acc[0:D_HEAD] / acc[D_HEAD:D_HEAD + 1] for _, acc in carry], axis=0)
    return o.T


def _fox_kernel(qx_ref, kx_ref, vt_ref, o_ref, *, t):
    qi = pl.program_id(2)
    causal = lax.broadcasted_iota(jnp.int32, (t, t), 0) <= lax.broadcasted_iota(jnp.int32, (t, t), 1)
    o = _flash_t([qx_ref[0, e] for e in range(2)], kx_ref, vt_ref, (0, 1), qi, t,
                 lambda st, kb: st, lambda st, kb: jnp.where(causal, st, -jnp.inf))
    o_ref[0] = o.astype(o_ref.dtype)


def _fox(qx, kx, vt, *, t):
    bsz, nh, s, _ = qx.shape
    nt = s // t
    return pl.pallas_call(
        functools.partial(_fox_kernel, t=t),
        grid=(bsz, nh // 2, nt),
        in_specs=[pl.BlockSpec((1, 2, t, LANES), lambda b, j, i: (b, j, i, 0)),
                  pl.BlockSpec((1, 2, s, LANES), lambda b, j, i: (b, j, 0, 0)),
                  pl.BlockSpec((1, 2, nt, V_ROWS, t), lambda b, j, i: (b, j, 0, 0, 0))],
        out_specs=pl.BlockSpec((1, t, LANES), lambda b, j, i: (b, i, j)),
        out_shape=jax.ShapeDtypeStruct((bsz, s, nh * D_HEAD), BF16),
        compiler_params=_cparams(("arbitrary", "arbitrary", "arbitrary")),
        name="fox_attn",
    )(qx, kx, vt)


def _dsa_kernel(qx_ref, iq_ref, auxt_ref, kx_ref, vt_ref, ik_ref, o_ref, key_ref, thr_ref,
                *, t, n_heads, k_top):
    qi = pl.program_id(1)
    n_kb = qi + 1
    kidx = lax.broadcasted_iota(jnp.int32, (t, t), 0)
    qidx = lax.broadcasted_iota(jnp.int32, (t, t), 1)
    lane = lax.broadcasted_iota(jnp.int32, (t, LANES), 1)
    iw = auxt_ref[0, 0]

    iqm = []
    for h in range(N_IDX_HEADS):
        iq2 = iq_ref[0, :, (h // 2) * LANES:(h // 2 + 1) * LANES]
        keep = (lane < D_IDX) if h % 2 == 0 else (lane >= D_IDX)
        iqm.append(iq2 * jnp.where(keep, 1.0, 0.0).astype(BF16))

    def score_block(kb, masked):
        off = pl.multiple_of(kb * t, t)
        ik2 = ik_ref[0, pl.ds(off, t), :]
        sc = jnp.zeros((t, t), F32)
        for h in range(N_IDX_HEADS):
            d = lax.dot_general(ik2, iqm[h], NT_DIMS, preferred_element_type=F32)
            sc = sc + jnp.maximum(d, 0.0) * iw[h:h + 1, :]
        if masked:
            sc = jnp.where(kidx <= qidx, sc, -jnp.inf)
        bits = pltpu.bitcast(sc, jnp.int32)
        key = bits ^ ((bits >> 31) & 0x7FFFFFFF)
        key_ref[kb] = jnp.where(key == -1, 0, key)

    def plain_score(kb, _):
        score_block(kb, False)
        return 0

    lax.fori_loop(0, qi, plain_score, 0)
    score_block(qi, True)

    def count(cmp_fn):
        def body(kb, tot):
            return tot + jnp.sum(jnp.where(cmp_fn(key_ref[kb]), 1.0, 0.0), axis=0, keepdims=True)
        return lax.fori_loop(0, n_kb, body, jnp.zeros((1, t), F32))

    def search(it, u):
        bit = lax.shift_left(jnp.int32(1), 31 - it)
        cand = (u | bit) ^ INT_MIN
        return jnp.where(count(lambda kk: kk >= cand) >= k_top, u | bit, u)

    u = lax.fori_loop(0, 32, search, jnp.zeros((1, t), jnp.int32))
    thr_ref[...] = jnp.broadcast_to(jnp.maximum(u ^ INT_MIN, KEY_NEG_INF + 1), thr_ref.shape)
    thr = thr_ref[0:1, :]
    has_ties = jnp.max(count(lambda kk: kk >= thr)) > k_top

    @pl.when(jnp.logical_not(has_ties))
    def _():
        def body(kb, _):
            key_ref[kb] = jnp.where(key_ref[kb] >= thr, 0, NEG_INF_BITS)
            return 0
        lax.fori_loop(0, n_kb, body, 0)

    @pl.when(has_ties)
    def _():
        need = k_top - count(lambda kk: kk > thr)
        earlier = jnp.where(qidx < kidx, 1.0, 0.0).astype(BF16)

        def body(kb, run):
            kk = key_ref[kb]
            eq = jnp.where(kk == thr, 1.0, 0.0)
            rank = jnp.dot(earlier, eq.astype(BF16), preferred_element_type=F32) + run
            sel = jnp.where(kk > thr, 1.0, jnp.where(rank < need, eq, 0.0))
            key_ref[kb] = jnp.where(sel > 0.5, 0, NEG_INF_BITS)
            return run + jnp.sum(eq, axis=0, keepdims=True)
        lax.fori_loop(0, n_kb, body, jnp.zeros((1, t), F32))

    add_mask = lambda st, kb: st + pltpu.bitcast(key_ref[kb], F32)
    for j in range(n_heads // 2):
        heads = (2 * j, 2 * j + 1)
        o = _flash_t([qx_ref[0, h] for h in heads], kx_ref, vt_ref, heads, qi, t, add_mask, add_mask)
        o_ref[0, :, j * LANES:(j + 1) * LANES] = o.astype(o_ref.dtype)


def _dsa(qx, iq, auxt, kx, vt, ik, *, t, k_top):
    bsz, nh, s, _ = qx.shape
    nt = s // t
    return pl.pallas_call(
        functools.partial(_dsa_kernel, t=t, n_heads=nh, k_top=k_top),
        grid=(bsz, nt),
        in_specs=[pl.BlockSpec((1, nh, t, LANES), lambda b, i: (b, 0, i, 0)),
                  pl.BlockSpec((1, t, N_IDX_HEADS * D_IDX), lambda b, i: (b, i, 0)),
                  pl.BlockSpec((1, 1, SUBLANES, t), lambda b, i: (b, i, 0, 0)),
                  pl.BlockSpec((1, nh, s, LANES), lambda b, i: (b, 0, 0, 0)),
                  pl.BlockSpec((1, nh, nt, V_ROWS, t), lambda b, i: (b, 0, 0, 0, 0)),
                  pl.BlockSpec((1, s, LANES), lambda b, i: (b, 0, 0))],
        out_specs=pl.BlockSpec((1, t, nh * D_HEAD), lambda b, i: (b, i, 0)),
        out_shape=jax.ShapeDtypeStruct((bsz, s, nh * D_HEAD), BF16),
        scratch_shapes=[pltpu.VMEM((nt, t, t), jnp.int32), pltpu.VMEM((SUBLANES, t), jnp.int32)],
        compiler_params=_cparams(("arbitrary", "arbitrary")),
        name="dsa_attn",
    )(qx, iq, auxt, kx, vt, ik)


def _ffn_kernel(x_ref, yd_ref, yf_ref, dg_ref, xg_ref, g1_ref, sh_ref, sc_ref, g2_ref, ng_ref, fg_ref,
                wo_ref, wg_ref, wu_ref, wd_ref, o_ref, x1_ref, h_ref, acc_ref, *, hd):
    c = pl.program_id(2)
    dot = functools.partial(jnp.dot, preferred_element_type=F32)

    @pl.when(c == 0)
    def _():
        yd = _rms(yd_ref[0].astype(F32), dg_ref[...]).astype(BF16)
        yf = _rms(yf_ref[0].astype(F32), xg_ref[...]).astype(BF16)
        mix = dot(yd, wo_ref[0:hd, :]) + dot(yf, wo_ref[hd:, :])
        x1 = x_ref[0] + g1_ref[0] * mix
        x1_ref[...] = x1
        h_ref[...] = (_rms(x1, ng_ref[...]) * (1.0 + sc_ref[0]) + sh_ref[0]).astype(BF16)
        acc_ref[...] = jnp.zeros_like(acc_ref)

    h = h_ref[...]
    gate = dot(h, wg_ref[0])
    up = dot(h, wu_ref[0])
    act = gate * jax.nn.sigmoid(gate) * up
    acc_ref[...] += dot(act.astype(BF16), wd_ref[0])

    @pl.when(c == pl.num_programs(2) - 1)
    def _():
        x2 = x1_ref[...] + g2_ref[0] * acc_ref[...]
        o_ref[0] = _rms(x2, fg_ref[...])


def _ffn(x, yd, yf, dg, xg, g1, sh, sc, g2, ng, fg, wo, wg, wu, wd, *, tm):
    bsz, s, d = x.shape
    hd = yd.shape[-1]
    n_chunks, _, fc = wg.shape
    tok = lambda width: pl.BlockSpec((1, tm, width), lambda b, i, c: (b, i, 0))
    per_b = pl.BlockSpec((1, 1, d), lambda b, i, c: (b, 0, 0))
    vec = pl.BlockSpec((1, d), lambda b, i, c: (0, 0))
    gvec = pl.BlockSpec((1, hd), lambda b, i, c: (0, 0))
    return pl.pallas_call(
        functools.partial(_ffn_kernel, hd=hd),
        grid=(bsz, s // tm, n_chunks),
        in_specs=[tok(d), tok(hd), tok(hd), gvec, gvec, per_b, per_b, per_b, per_b, vec, vec,
                  pl.BlockSpec((2 * hd, d), lambda b, i, c: (0, 0)),
                  pl.BlockSpec((1, d, fc), lambda b, i, c: (c, 0, 0)),
                  pl.BlockSpec((1, d, fc), lambda b, i, c: (c, 0, 0)),
                  pl.BlockSpec((1, fc, d), lambda b, i, c: (c, 0, 0))],
        out_specs=tok(d),
        out_shape=jax.ShapeDtypeStruct((bsz, s, d), F32),
        scratch_shapes=[pltpu.VMEM((tm, d), F32), pltpu.VMEM((tm, d), BF16), pltpu.VMEM((tm, d), F32)],
        compiler_params=_cparams(("arbitrary", "arbitrary", "arbitrary")),
        name="outproj_ffn",
    )(x, yd, yf, dg, xg, g1, sh, sc, g2, ng, fg, wo, wg, wu, wd)


def _pick_tile(n, target):
    t = min(n, target)
    while n % t:
        t //= 2
    return t


def _ff_chunk(d_ff):
    units = d_ff // LANES
    best = 1
    for u in range(1, units + 1):
        if units % u == 0 and u * LANES <= 1536:
            best = u
    return best * LANES


def kernel(x, c, positions, ada_w, ada_b, norm_attn_g, w_in, b_forget, dsa_norm_g, fox_norm_g,
           w_out, norm_ffn_g, w_gate, w_up, w_down, final_norm_g):
    bsz, s, d = x.shape
    depth = ada_w.shape[0]
    n_fox = b_forget.shape[1]
    hd = dsa_norm_g.shape[1]
    assert depth == 1, "the fused final RMSNorm assumes a single layer"
    assert n_fox == hd // D_HEAD == 8 and hd % LANES == 0
    k_top = min(TOP_K_MAX, s // 4)
    t_attn = _pick_tile(s, 512)
    tm_in = _pick_tile(s, 512)
    tm_ffn = _pick_tile(s, 512)

    sizes = (hd, hd, hd, N_IDX_HEADS * D_IDX, D_IDX, N_IDX_HEADS, hd, hd, hd, n_fox)
    pts = np.cumsum((0,) + sizes)
    wl = w_in[0]
    seg = lambda k: wl[:, pts[k]:pts[k + 1]]
    small = jnp.zeros((d, LANES), F32)
    small = small.at[:, 0:N_IDX_HEADS].set(seg(5)).at[:, 8:8 + n_fox].set(seg(9))
    w_cat = jnp.concatenate([seg(0), seg(1), seg(2), seg(6), seg(7), seg(8),
                             seg(3), seg(4), seg(4), small], axis=1).astype(BF16)
    bfv = jnp.zeros((1, LANES), F32).at[0, 8:8 + n_fox].set(b_forget[0])

    lane = jnp.arange(LANES)
    inv_freq = ROPE_THETA ** (-jnp.arange(0, ROT_DIM, 2, dtype=F32) / ROT_DIM)
    invf = jnp.where((lane % D_HEAD) < ROT_DIM, inv_freq[lane % (ROT_DIM // 2)], 0.0).reshape(1, LANES)
    pos = positions.reshape(bsz, s, 1)

    mod = _modulation(c, ada_w[0], ada_b[0])
    sh1, sc1, g1, sh2, sc2, g2 = [m.reshape(bsz, 1, d) for m in jnp.split(mod, N_MOD, axis=-1)]

    qxd, kxd, vtd, qxf, kxf, vtf, iq, ik, auxt = _inproj(
        x, sh1, sc1, norm_attn_g[0].reshape(1, d), pos, invf, w_cat, bfv, hd=hd, tm=tm_in, tk=t_attn)

    yf = _fox(qxf, kxf, vtf, t=t_attn)
    yd = _dsa(qxd, iq, auxt, kxd, vtd, ik, t=t_attn, k_top=k_top)

    d_ff = w_gate.shape[2]
    fc = _ff_chunk(d_ff)
    n_chunks = d_ff // fc
    wg = w_gate[0].astype(BF16).reshape(d, n_chunks, fc).transpose(1, 0, 2)
    wu = w_up[0].astype(BF16).reshape(d, n_chunks, fc).transpose(1, 0, 2)
    wd = w_down[0].astype(BF16).reshape(n_chunks, fc, d)
    return _ffn(x, yd, yf, dsa_norm_g[0].reshape(1, hd), fox_norm_g[0].reshape(1, hd),
                g1, sh2, sc2, g2, norm_ffn_g[0].reshape(1, d), final_norm_g.reshape(1, d),
                w_out[0].astype(BF16), wg, wu, wd, tm=tm_ffn)
```

```python
import functools

import numpy as np
import jax
import jax.numpy as jnp
from jax import lax
from jax.experimental import pallas as pl
from jax.experimental.pallas import tpu as pltpu

F32 = jnp.float32
BF16 = jnp.bfloat16

D_HEAD = 64
N_IDX_HEADS = 4
D_IDX = 64
TOP_K_MAX = 256
ROPE_THETA = 500000.0
ROT_DIM = D_HEAD // 4
N_MOD = 6
RMS_EPS = 1e-6

LOG2E = 1.4426950408889634
V_ROWS = 80
LANES = 128
SUBLANES = 8
VMEM_LIMIT = 56 * 1024 * 1024

F32_MAX = float(np.finfo(np.float32).max)
F32_TINY = float(np.finfo(np.float32).tiny)
MAX_NARROW_STEPS = 48
INT_MIN = -(2 ** 31)
KEY_NEG_INF = int(np.int32(np.uint32(0xFF800000) ^ np.uint32(0x7FFFFFFF)))

NT_DIMS = (((1,), (1,)), ((), ()))


def _cparams(sem):
    return pltpu.CompilerParams(dimension_semantics=sem, vmem_limit_bytes=VMEM_LIMIT)


def _split3(a):
    a0 = a.astype(BF16).astype(F32)
    a1 = (a - a0).astype(BF16).astype(F32)
    return a0, a1, a - a0 - a1


def _rms(x, g):
    return x * lax.rsqrt(jnp.mean(x * x, axis=-1, keepdims=True) + RMS_EPS) * g


def _mod_kernel(c_ref, w_ref, b_ref, o_ref):
    c = c_ref[...]
    a0, a1, _ = _split3(c * jax.nn.sigmoid(c))
    w0, w1, _ = _split3(w_ref[...])
    dot = lambda a, b: jnp.dot(a.astype(BF16), b.astype(BF16), preferred_element_type=F32)
    o_ref[...] = dot(a0, w0) + (dot(a0, w1) + dot(a1, w0)) + b_ref[...]


def _modulation(c, w, b):
    bsz, d = c.shape
    n = w.shape[1]
    tn = 1024
    return pl.pallas_call(
        _mod_kernel,
        grid=(n // tn,),
        in_specs=[pl.BlockSpec((bsz, d), lambda j: (0, 0)),
                  pl.BlockSpec((d, tn), lambda j: (0, j)),
                  pl.BlockSpec((1, tn), lambda j: (0, j))],
        out_specs=pl.BlockSpec((bsz, tn), lambda j: (0, j)),
        out_shape=jax.ShapeDtypeStruct((bsz, n), F32),
        compiler_params=_cparams(("arbitrary",)),
        name="adaln_mod",
    )(c, w, b.reshape(1, n))


def _inproj_kernel(x_ref, sh_ref, sc_ref, g_ref, pos_ref, invf_ref, w_ref, bf_ref,
                   qxd_ref, kxd_ref, vtd_ref, qxf_ref, kxf_ref, vtf_ref, iq_ref, ik_ref, auxt_ref,
                   carry_ref, *, hd, tm, tk):
    i = pl.program_id(1)
    h = _rms(x_ref[0], g_ref[...]) * (1.0 + sc_ref[0]) + sh_ref[0]
    proj = jnp.dot(h.astype(BF16), w_ref[...], preferred_element_type=F32)

    lane = lax.broadcasted_iota(jnp.int32, (tm, LANES), 1)
    ang = pos_ref[0].astype(F32) * invf_ref[...]
    cosv = jnp.cos(ang)
    sinv = jnp.sin(ang)
    first_half = (lane & (D_HEAD - 1)) < (ROT_DIM // 2)
    sin_signed = jnp.where(first_half, -sinv, sinv)

    def rope(x):
        partner = jnp.where(first_half, pltpu.roll(x, LANES - ROT_DIM // 2, 1), pltpu.roll(x, ROT_DIM // 2, 1))
        return x * cosv + partner * sin_signed

    def slab(base, j):
        return proj[:, base + j * LANES: base + (j + 1) * LANES]

    def head_of(x, h):
        return pltpu.roll(x, D_HEAD, 1) if h % 2 else x

    def put_vt(ref, h, v_slab):
        one_row = jnp.where(lane == D_HEAD, 1.0, 0.0)
        vt = jnp.where(lane < D_HEAD, head_of(v_slab, h), one_row).T[0:V_ROWS, :]
        for u in range(tm // tk):
            ref[0, h, u] = vt[:, u * tk:(u + 1) * tk].astype(BF16)

    qscale = (D_HEAD ** -0.5) * LOG2E
    in_head = lane < D_HEAD

    for j in range(hd // LANES):
        rq = rope(slab(0, j)) * qscale
        rk = rope(slab(hd, j))
        for h in (2 * j, 2 * j + 1):
            qxd_ref[0, h] = jnp.where(in_head, head_of(rq, h), 0.0).astype(BF16)
            kxd_ref[0, h] = jnp.where(in_head, head_of(rk, h), 0.0).astype(BF16)
            put_vt(vtd_ref, h, slab(2 * hd, j))
    base = 6 * hd
    for j in range(N_IDX_HEADS * D_IDX // LANES):
        iq_ref[0, :, j * LANES:(j + 1) * LANES] = (rope(slab(base, j)) * (D_IDX ** -0.5)).astype(BF16)
    base += N_IDX_HEADS * D_IDX
    ik_ref[0] = rope(slab(base, 0)).astype(BF16)
    base += LANES

    small = slab(base, 0)
    z = small + bf_ref[...]
    logf = jnp.minimum(z, 0.0) - jnp.log1p(jnp.exp(-jnp.abs(z)))
    logf = jnp.where((lane >= 8) & (lane < 16), logf, 0.0)

    @pl.when(i == 0)
    def _():
        carry_ref[...] = jnp.zeros_like(carry_ref)

    r = lax.broadcasted_iota(jnp.int32, (tm, tm), 0)
    c = lax.broadcasted_iota(jnp.int32, (tm, tm), 1)
    tri = jnp.where(c <= r, 1.0, 0.0).astype(BF16)
    dot = lambda a, b: jnp.dot(a, b.astype(BF16), preferred_element_type=F32)
    l0, l1, l2 = _split3(logf)
    cum = dot(tri, l0) + (dot(tri, l1) + dot(tri, l2)) + carry_ref[...]
    carry_ref[...] = cum[tm - 1:tm, :]

    aux = jnp.where(lane < N_IDX_HEADS, small * (N_IDX_HEADS ** -0.5), 0.0)
    for u in range(tm // tk):
        auxt_ref[0, u] = aux[u * tk:(u + 1) * tk, :].T[0:SUBLANES, :]

    ncum = cum * (-LOG2E)
    ones3 = jnp.where(lane < D_HEAD + 3, 1.0, 0.0)
    for j in range(hd // LANES):
        for h in (2 * j, 2 * j + 1):
            qxf_ref[0, h] = jnp.where(in_head, head_of(slab(3 * hd, j), h) * qscale, ones3).astype(BF16)
            p0, p1, p2 = _split3(ncum[:, 8 + h:9 + h])
            extra = jnp.where(lane == D_HEAD, p0,
                              jnp.where(lane == D_HEAD + 1, p1, jnp.where(lane == D_HEAD + 2, p2, 0.0)))
            kxf_ref[0, h] = jnp.where(in_head, head_of(slab(4 * hd, j), h), extra).astype(BF16)
            put_vt(vtf_ref, h, slab(5 * hd, j))


def _inproj(x, sh, sc, g, pos, invf, w, bfv, *, hd, tm, tk):
    bsz, s, d = x.shape
    n_cols = w.shape[1]
    nh = hd // D_HEAD
    tok = lambda width: pl.BlockSpec((1, tm, width), lambda b, i: (b, i, 0))
    per_b = pl.BlockSpec((1, 1, d), lambda b, i: (b, 0, 0))
    const = lambda shape: pl.BlockSpec(shape, lambda b, i: (0,) * len(shape))
    headed = jax.ShapeDtypeStruct((bsz, nh, s, LANES), BF16)
    headed_spec = pl.BlockSpec((1, nh, tm, LANES), lambda b, i: (b, 0, i, 0))
    vt = jax.ShapeDtypeStruct((bsz, nh, s // tk, V_ROWS, tk), BF16)
    vt_spec = pl.BlockSpec((1, nh, tm // tk, V_ROWS, tk), lambda b, i: (b, 0, i, 0, 0))
    out_shape = [headed, headed, vt, headed, headed, vt,
                 jax.ShapeDtypeStruct((bsz, s, N_IDX_HEADS * D_IDX), BF16),
                 jax.ShapeDtypeStruct((bsz, s, LANES), BF16),
                 jax.ShapeDtypeStruct((bsz, s // tk, SUBLANES, tk), F32)]
    out_specs = [headed_spec, headed_spec, vt_spec, headed_spec, headed_spec, vt_spec,
                 tok(N_IDX_HEADS * D_IDX), tok(LANES),
                 pl.BlockSpec((1, tm // tk, SUBLANES, tk), lambda b, i: (b, i, 0, 0))]
    return pl.pallas_call(
        functools.partial(_inproj_kernel, hd=hd, tm=tm, tk=tk),
        grid=(bsz, s // tm),
        in_specs=[tok(d), per_b, per_b, const((1, d)), tok(1), const((1, LANES)),
                  const((d, n_cols)), const((1, LANES))],
        out_specs=out_specs,
        out_shape=out_shape,
        scratch_shapes=[pltpu.VMEM((1, LANES), F32)],
        compiler_params=_cparams(("arbitrary", "arbitrary")),
        name="inproj",
    )(x, sh, sc, g, pos, invf, w, bfv)


def _flash_t(q, kx_ref, vt_ref, heads, n_plain, t, plain_fn, last_fn, s_refs):
    def produce(e, kb, fn):
        off = pl.multiple_of(kb * t, t)
        st = fn(lax.dot_general(kx_ref[0, heads[e], pl.ds(off, t), :], q[e], NT_DIMS,
                                preferred_element_type=F32), kb)
        s_refs[e][...] = st
        return jnp.max(st, axis=0, keepdims=True)

    def consume(e, kb, state, bmax):
        m, acc = state
        m_new = jnp.maximum(m, bmax)
        pt = jnp.exp2(s_refs[e][...] - m_new).astype(BF16)
        acc = jnp.exp2(m - m_new) * acc + jnp.dot(vt_ref[0, heads[e], kb], pt, preferred_element_type=F32)
        return m_new, acc

    def body(kb, carry):
        states, bmaxes = carry
        cur = jnp.where(kb == 0, n_plain, kb - 1)
        new_states, new_bmaxes = [], []
        for e in range(len(heads)):
            new_states.append(consume(e, cur, states[e], bmaxes[e]))
            new_bmaxes.append(produce(e, kb, plain_fn))
        return tuple(new_states), tuple(new_bmaxes)

    states = tuple((jnp.full((1, t), -1e30, F32), jnp.zeros((V_ROWS, t), F32)) for _ in heads)
    bmaxes = tuple(produce(e, n_plain, last_fn) for e in range(len(heads)))
    states, bmaxes = lax.fori_loop(0, n_plain, body, (states, bmaxes))
    cur = jnp.maximum(n_plain - 1, 0)
    states = [consume(e, cur, states[e], bmaxes[e]) for e in range(len(heads))]
    o = jnp.concatenate([acc[0:D_HEAD] / acc[D_HEAD:D_HEAD + 1] for _, acc in states], axis=0)
    return o.T


def _fox_kernel(qx_ref, kx_ref, vt_ref, o_ref, *s_refs, t, hg):
    qi = pl.program_id(2)
    causal = lax.broadcasted_iota(jnp.int32, (t, t), 0) <= lax.broadcasted_iota(jnp.int32, (t, t), 1)
    o = _flash_t([qx_ref[0, e] for e in range(hg)], kx_ref, vt_ref, tuple(range(hg)), qi, t,
                 lambda st, kb: st, lambda st, kb: jnp.where(causal, st, -jnp.inf), s_refs)
    o_ref[0] = o.astype(o_ref.dtype)


def _fox(qx, kx, vt, *, t, hg):
    bsz, nh, s, _ = qx.shape
    nt = s // t
    return pl.pallas_call(
        functools.partial(_fox_kernel, t=t, hg=hg),
        grid=(bsz, nh // hg, nt),
        in_specs=[pl.BlockSpec((1, hg, t, LANES), lambda b, j, i: (b, j, i, 0)),
                  pl.BlockSpec((1, hg, s, LANES), lambda b, j, i: (b, j, 0, 0)),
                  pl.BlockSpec((1, hg, nt, V_ROWS, t), lambda b, j, i: (b, j, 0, 0, 0))],
        out_specs=pl.BlockSpec((1, t, hg * D_HEAD), lambda b, j, i: (b, i, j)),
        out_shape=jax.ShapeDtypeStruct((bsz, s, nh * D_HEAD), BF16),
        scratch_shapes=[pltpu.VMEM((t, t), F32)] * hg,
        compiler_params=_cparams(("arbitrary", "arbitrary", "arbitrary")),
        name="fox_attn",
    )(qx, kx, vt)


def _dsa_kernel(qx_ref, iq_ref, auxt_ref, kx_ref, vt_ref, ik_ref, o_ref, sc_ref, thr_ref, *s_refs,
                t, n_heads, k_top):
    qi = pl.program_id(1)
    n_kb = qi + 1
    kidx = lax.broadcasted_iota(jnp.int32, (t, t), 0)
    qidx = lax.broadcasted_iota(jnp.int32, (t, t), 1)
    lane = lax.broadcasted_iota(jnp.int32, (t, LANES), 1)
    iw = auxt_ref[0, 0]

    iqm = []
    for h in range(N_IDX_HEADS):
        iq2 = iq_ref[0, :, (h // 2) * LANES:(h // 2 + 1) * LANES]
        keep = (lane < D_IDX) if h % 2 == 0 else (lane >= D_IDX)
        iqm.append(iq2 * jnp.where(keep, 1.0, 0.0).astype(BF16))

    def score_block(kb, masked, lohi):
        off = pl.multiple_of(kb * t, t)
        ik2 = ik_ref[0, pl.ds(off, t), :]
        sc = jnp.zeros((t, t), F32)
        for h in range(N_IDX_HEADS):
            d = lax.dot_general(ik2, iqm[h], NT_DIMS, preferred_element_type=F32)
            sc = sc + jnp.maximum(d, 0.0) * iw[h:h + 1, :]
        sc_min = sc
        if masked:
            causal = kidx <= qidx
            sc_min = jnp.where(causal, sc, jnp.inf)
            sc = jnp.where(causal, sc, -jnp.inf)
        sc_ref[kb] = sc
        return (jnp.minimum(lohi[0], jnp.min(sc_min, axis=0, keepdims=True)),
                jnp.maximum(lohi[1], jnp.max(sc, axis=0, keepdims=True)))

    lohi = (jnp.full((1, t), jnp.inf, F32), jnp.full((1, t), -jnp.inf, F32))
    lohi = lax.fori_loop(0, qi, lambda kb, c: score_block(kb, False, c), lohi)
    smin, smax = score_block(qi, True, lohi)

    def count(cmp_fn):
        def body(kb, tot):
            return tot + jnp.sum(jnp.where(cmp_fn(sc_ref[kb]), 1.0, 0.0), axis=0, keepdims=True)
        return lax.fori_loop(0, n_kb, body, jnp.zeros((1, t), F32))

    kf = float(k_top)
    n_causal = (qi * t + 1 + lax.broadcasted_iota(jnp.int32, (1, t), 1)).astype(F32)
    c0_ge = count(lambda x: x >= 0.0)
    c0_gt = count(lambda x: x > 0.0)
    few = n_causal <= kf
    positive = c0_gt >= kf
    at_zero = jnp.logical_and(jnp.logical_not(positive), c0_ge >= kf)
    hi_pos = smax + jnp.maximum(jnp.abs(smax) * 1e-6, 1e-30)
    lo = jnp.where(few, -F32_MAX, jnp.where(at_zero, 0.0, jnp.where(positive, F32_TINY, smin)))
    c_lo = jnp.where(few, n_causal, jnp.where(at_zero, c0_ge, jnp.where(positive, c0_gt, n_causal)))
    hi = jnp.where(positive, hi_pos, -F32_TINY)
    c_hi = jnp.where(positive, 0.0, c0_ge)
    done = jnp.where(jnp.logical_or(jnp.logical_or(few, at_zero), c_lo <= kf), 1.0, 0.0)

    def narrow(carry):
        it, _, lo, hi, c_lo, c_hi, done, bisect = carry
        mid = 0.5 * lo + 0.5 * hi
        stuck = jnp.logical_or(mid <= lo, mid >= hi)
        a, b = jnp.log(jnp.maximum(c_lo, 0.5)), jnp.log(jnp.maximum(c_hi, 0.5))
        frac = jnp.clip((a - np.log(kf)) / jnp.maximum(a - b, 1e-6), 0.02, 0.98)
        interp = lo + (hi - lo) * frac
        use_mid = jnp.logical_or(bisect > 0.5, jnp.logical_or(interp <= lo, interp >= hi))
        cand = jnp.where(use_mid, mid, interp)
        c = count(lambda x: x >= cand)
        active = jnp.logical_and(done < 0.5, jnp.logical_not(stuck))
        up = jnp.logical_and(active, c >= kf)
        down = jnp.logical_and(active, c < kf)
        inside = c_lo - c_hi
        lo, c_lo = jnp.where(up, cand, lo), jnp.where(up, c, c_lo)
        hi, c_hi = jnp.where(down, cand, hi), jnp.where(down, c, c_hi)
        done = jnp.where(jnp.logical_or(stuck, c_lo <= kf), 1.0, done)
        slow = jnp.logical_and(jnp.logical_not(use_mid), c_lo - c_hi > 0.5 * inside)
        return it + 1, jnp.min(done), lo, hi, c_lo, c_hi, done, jnp.where(slow, 1.0, 0.0)

    carry = (jnp.int32(0), jnp.min(done), lo, hi, c_lo, c_hi, done, jnp.zeros((1, t), F32))
    carry = lax.while_loop(lambda c: jnp.logical_and(c[0] < MAX_NARROW_STEPS, c[1] < 0.5), narrow, carry)
    _, all_done, lo, _, c_lo, _, _, _ = carry
    thr_ref[0:1, :] = lo
    thr_ref[1:2, :] = c_lo

    @pl.when(all_done < 0.5)
    def _():
        def to_float(key):
            return pltpu.bitcast(key ^ ((key >> 31) & 0x7FFFFFFF), F32)

        def search(it, u):
            bit = lax.shift_left(jnp.int32(1), 31 - it)
            cand = to_float((u | bit) ^ INT_MIN)
            return jnp.where(count(lambda x: x >= cand) >= kf, u | bit, u)

        u = lax.fori_loop(0, 32, search, jnp.zeros((1, t), jnp.int32))
        thr_bits = to_float(jnp.maximum(u ^ INT_MIN, KEY_NEG_INF + 1))
        thr_ref[0:1, :] = thr_bits
        thr_ref[1:2, :] = count(lambda x: x >= thr_bits)

    thr = thr_ref[0:1, :]
    has_ties = jnp.max(thr_ref[1:2, :]) > kf

    @pl.when(jnp.logical_not(has_ties))
    def _():
        def body(kb, _):
            sc_ref[kb] = jnp.where(sc_ref[kb] >= thr, 0.0, -jnp.inf)
            return 0
        lax.fori_loop(0, n_kb, body, 0)

    @pl.when(has_ties)
    def _():
        need = kf - count(lambda x: x > thr)
        earlier = jnp.where(qidx < kidx, 1.0, 0.0).astype(BF16)

        def body(kb, run):
            x = sc_ref[kb]
            eq = jnp.where(x == thr, 1.0, 0.0)
            rank = jnp.dot(earlier, eq.astype(BF16), preferred_element_type=F32) + run
            sel = jnp.where(x > thr, 1.0, jnp.where(rank < need, eq, 0.0))
            sc_ref[kb] = jnp.where(sel > 0.5, 0.0, -jnp.inf)
            return run + jnp.sum(eq, axis=0, keepdims=True)
        lax.fori_loop(0, n_kb, body, jnp.zeros((1, t), F32))

    add_mask = lambda st, kb: st + sc_ref[kb]
    hg = len(s_refs)
    for j in range(n_heads // hg):
        heads = tuple(range(j * hg, (j + 1) * hg))
        o = _flash_t([qx_ref[0, h] for h in heads], kx_ref, vt_ref, heads, qi, t, add_mask, add_mask, s_refs)
        o_ref[0, :, j * hg * D_HEAD:(j + 1) * hg * D_HEAD] = o.astype(o_ref.dtype)


def _dsa(qx, iq, auxt, kx, vt, ik, *, t, k_top, hg):
    bsz, nh, s, _ = qx.shape
    nt = s // t
    return pl.pallas_call(
        functools.partial(_dsa_kernel, t=t, n_heads=nh, k_top=k_top),
        grid=(bsz, nt),
        in_specs=[pl.BlockSpec((1, nh, t, LANES), lambda b, i: (b, 0, i, 0)),
                  pl.BlockSpec((1, t, N_IDX_HEADS * D_IDX), lambda b, i: (b, i, 0)),
                  pl.BlockSpec((1, 1, SUBLANES, t), lambda b, i: (b, i, 0, 0)),
                  pl.BlockSpec((1, nh, s, LANES), lambda b, i: (b, 0, 0, 0)),
                  pl.BlockSpec((1, nh, nt, V_ROWS, t), lambda b, i: (b, 0, 0, 0, 0)),
                  pl.BlockSpec((1, s, LANES), lambda b, i: (b, 0, 0))],
        out_specs=pl.BlockSpec((1, t, nh * D_HEAD), lambda b, i: (b, i, 0)),
        out_shape=jax.ShapeDtypeStruct((bsz, s, nh * D_HEAD), BF16),
        scratch_shapes=[pltpu.VMEM((nt, t, t), F32), pltpu.VMEM((SUBLANES, t), F32)]
        + [pltpu.VMEM((t, t), F32)] * hg,
        compiler_params=_cparams(("arbitrary", "arbitrary")),
        name="dsa_attn",
    )(qx, iq, auxt, kx, vt, ik)


def _ffn_kernel(x_ref, yd_ref, yf_ref, dg_ref, xg_ref, g1_ref, sh_ref, sc_ref, g2_ref, ng_ref, fg_ref,
                wo_ref, wg_ref, wu_ref, wd_ref, o_ref, x1_ref, h_ref, acc_ref, *, hd):
    c = pl.program_id(2)
    dot = functools.partial(jnp.dot, preferred_element_type=F32)

    @pl.when(c == 0)
    def _():
        yd = _rms(yd_ref[0].astype(F32), dg_ref[...]).astype(BF16)
        yf = _rms(yf_ref[0].astype(F32), xg_ref[...]).astype(BF16)
        mix = dot(yd, wo_ref[0:hd, :]) + dot(yf, wo_ref[hd:, :])
        x1 = x_ref[0] + g1_ref[0] * mix
        x1_ref[...] = x1
        h_ref[...] = (_rms(x1, ng_ref[...]) * (1.0 + sc_ref[0]) + sh_ref[0]).astype(BF16)
        acc_ref[...] = jnp.zeros_like(acc_ref)

    h = h_ref[...]
    gate = dot(h, wg_ref[0])
    up = dot(h, wu_ref[0])
    act = gate * jax.nn.sigmoid(gate) * up
    acc_ref[...] += dot(act.astype(BF16), wd_ref[0])

    @pl.when(c == pl.num_programs(2) - 1)
    def _():
        x2 = x1_ref[...] + g2_ref[0] * acc_ref[...]
        o_ref[0] = _rms(x2, fg_ref[...])


def _ffn(x, yd, yf, dg, xg, g1, sh, sc, g2, ng, fg, wo, wg, wu, wd, *, tm):
    bsz, s, d = x.shape
    hd = yd.shape[-1]
    n_chunks, _, fc = wg.shape
    tok = lambda width: pl.BlockSpec((1, tm, width), lambda b, i, c: (b, i, 0))
    per_b = pl.BlockSpec((1, 1, d), lambda b, i, c: (b, 0, 0))
    vec = pl.BlockSpec((1, d), lambda b, i, c: (0, 0))
    gvec = pl.BlockSpec((1, hd), lambda b, i, c: (0, 0))
    return pl.pallas_call(
        functools.partial(_ffn_kernel, hd=hd),
        grid=(bsz, s // tm, n_chunks),
        in_specs=[tok(d), tok(hd), tok(hd), gvec, gvec, per_b, per_b, per_b, per_b, vec, vec,
                  pl.BlockSpec((2 * hd, d), lambda b, i, c: (0, 0)),
                  pl.BlockSpec((1, d, fc), lambda b, i, c: (c, 0, 0)),
                  pl.BlockSpec((1, d, fc), lambda b, i, c: (c, 0, 0)),
                  pl.BlockSpec((1, fc, d), lambda b, i, c: (c, 0, 0))],
        out_specs=tok(d),
        out_shape=jax.ShapeDtypeStruct((bsz, s, d), F32),
        scratch_shapes=[pltpu.VMEM((tm, d), F32), pltpu.VMEM((tm, d), BF16), pltpu.VMEM((tm, d), F32)],
        compiler_params=_cparams(("arbitrary", "arbitrary", "arbitrary")),
        name="outproj_ffn",
    )(x, yd, yf, dg, xg, g1, sh, sc, g2, ng, fg, wo, wg, wu, wd)


def _pick_tile(n, target):
    t = min(n, target)
    while n % t:
        t //= 2
    return t


def _ff_chunk(d_ff):
    units = d_ff // LANES
    best = 1
    for u in range(1, units + 1):
        if units % u == 0 and u * LANES <= 1536:
            best = u
    return best * LANES


def kernel(x, c, positions, ada_w, ada_b, norm_attn_g, w_in, b_forget, dsa_norm_g, fox_norm_g,
           w_out, norm_ffn_g, w_gate, w_up, w_down, final_norm_g):
    bsz, s, d = x.shape
    depth = ada_w.shape[0]
    n_fox = b_forget.shape[1]
    hd = dsa_norm_g.shape[1]
    assert depth == 1, "the fused final RMSNorm assumes a single layer"
    assert n_fox == hd // D_HEAD == 8 and hd % LANES == 0
    k_top = min(TOP_K_MAX, s // 4)
    t_attn = _pick_tile(s, 512)
    tm_in = _pick_tile(s, 512)
    tm_ffn = _pick_tile(s, 512)

    sizes = (hd, hd, hd, N_IDX_HEADS * D_IDX, D_IDX, N_IDX_HEADS, hd, hd, hd, n_fox)
    pts = np.cumsum((0,) + sizes)
    wl = w_in[0]
    seg = lambda k: wl[:, pts[k]:pts[k + 1]]
    small = jnp.zeros((d, LANES), F32)
    small = small.at[:, 0:N_IDX_HEADS].set(seg(5)).at[:, 8:8 + n_fox].set(seg(9))
    w_cat = jnp.concatenate([seg(0), seg(1), seg(2), seg(6), seg(7), seg(8),
                             seg(3), seg(4), seg(4), small], axis=1).astype(BF16)
    bfv = jnp.zeros((1, LANES), F32).at[0, 8:8 + n_fox].set(b_forget[0])

    lane = jnp.arange(LANES)
    inv_freq = ROPE_THETA ** (-jnp.arange(0, ROT_DIM, 2, dtype=F32) / ROT_DIM)
    invf = jnp.where((lane % D_HEAD) < ROT_DIM, inv_freq[lane % (ROT_DIM // 2)], 0.0).reshape(1, LANES)
    pos = positions.reshape(bsz, s, 1)

    mod = _modulation(c, ada_w[0], ada_b[0])
    sh1, sc1, g1, sh2, sc2, g2 = [m.reshape(bsz, 1, d) for m in jnp.split(mod, N_MOD, axis=-1)]

    qxd, kxd, vtd, qxf, kxf, vtf, iq, ik, auxt = _inproj(
        x, sh1, sc1, norm_attn_g[0].reshape(1, d), pos, invf, w_cat, bfv, hd=hd, tm=tm_in, tk=t_attn)

    yf = _fox(qxf, kxf, vtf, t=t_attn, hg=4)
    yd = _dsa(qxd, iq, auxt, kxd, vtd, ik, t=t_attn, k_top=k_top, hg=4)

    d_ff = w_gate.shape[2]
    fc = _ff_chunk(d_ff)
    n_chunks = d_ff // fc
    wg = w_gate[0].astype(BF16).reshape(d, n_chunks, fc).transpose(1, 0, 2)
    wu = w_up[0].astype(BF16).reshape(d, n_chunks, fc).transpose(1, 0, 2)
    wd = w_down[0].astype(BF16).reshape(n_chunks, fc, d)
    return _ffn(x, yd, yf, dsa_norm_g[0].reshape(1, hd), fox_norm_g[0].reshape(1, hd),
                g1, sh2, sc2, g2, norm_ffn_g[0].reshape(1, d), final_norm_g.reshape(1, d),
                w_out[0].astype(BF16), wg, wu, wd, tm=tm_ffn)
```

```python
import functools

import numpy as np
import jax
import jax.numpy as jnp
from jax import lax
from jax.experimental import pallas as pl
from jax.experimental.pallas import tpu as pltpu

F32 = jnp.float32
BF16 = jnp.bfloat16

D_HEAD = 64
N_IDX_HEADS = 4
D_IDX = 64
TOP_K_MAX = 256
ROPE_THETA = 500000.0
ROT_DIM = D_HEAD // 4
N_MOD = 6
RMS_EPS = 1e-6

LOG2E = 1.4426950408889634
V_ROWS = 80
LANES = 128
SUBLANES = 8
VMEM_LIMIT = 56 * 1024 * 1024

F32_MAX = float(np.finfo(np.float32).max)
F32_TINY = float(np.finfo(np.float32).tiny)
MAX_NARROW_STEPS = 48
INT_MIN = -(2 ** 31)
KEY_NEG_INF = int(np.int32(np.uint32(0xFF800000) ^ np.uint32(0x7FFFFFFF)))

NT_DIMS = (((1,), (1,)), ((), ()))


def _cparams(sem):
    return pltpu.CompilerParams(dimension_semantics=sem, vmem_limit_bytes=VMEM_LIMIT)


def _split3(a):
    a0 = a.astype(BF16).astype(F32)
    a1 = (a - a0).astype(BF16).astype(F32)
    return a0, a1, a - a0 - a1


def _rms(x, g):
    return x * lax.rsqrt(jnp.mean(x * x, axis=-1, keepdims=True) + RMS_EPS) * g


def _mod_kernel(c_ref, w_ref, b_ref, o_ref):
    c = c_ref[...]
    a0, a1, _ = _split3(c * jax.nn.sigmoid(c))
    w0, w1, _ = _split3(w_ref[...])
    dot = lambda a, b: jnp.dot(a.astype(BF16), b.astype(BF16), preferred_element_type=F32)
    o_ref[...] = dot(a0, w0) + (dot(a0, w1) + dot(a1, w0)) + b_ref[...]


def _modulation(c, w, b):
    bsz, d = c.shape
    n = w.shape[1]
    tn = 1024
    return pl.pallas_call(
        _mod_kernel,
        grid=(n // tn,),
        in_specs=[pl.BlockSpec((bsz, d), lambda j: (0, 0)),
                  pl.BlockSpec((d, tn), lambda j: (0, j)),
                  pl.BlockSpec((1, tn), lambda j: (0, j))],
        out_specs=pl.BlockSpec((bsz, tn), lambda j: (0, j)),
        out_shape=jax.ShapeDtypeStruct((bsz, n), F32),
        compiler_params=_cparams(("arbitrary",)),
        name="adaln_mod",
    )(c, w, b.reshape(1, n))


def _inproj_kernel(x_ref, sh_ref, sc_ref, g_ref, pos_ref, invf_ref, w_ref, bf_ref,
                   qxd_ref, kxd_ref, vtd_ref, qxf_ref, kxf_ref, vtf_ref, iq_ref, ik_ref, auxt_ref,
                   carry_ref, *, hd, tm, tk, th):
    i = pl.program_id(1)

    @pl.when(i == 0)
    def _():
        carry_ref[...] = jnp.zeros_like(carry_ref)

    lane = lax.broadcasted_iota(jnp.int32, (th, LANES), 1)
    first_half = (lane & (D_HEAD - 1)) < (ROT_DIM // 2)
    in_head = lane < D_HEAD
    one_row = jnp.where(lane == D_HEAD, 1.0, 0.0)
    ones3 = jnp.where(lane < D_HEAD + 3, 1.0, 0.0)
    r = lax.broadcasted_iota(jnp.int32, (th, th), 0)
    c = lax.broadcasted_iota(jnp.int32, (th, th), 1)
    tri = jnp.where(c <= r, 1.0, 0.0).astype(BF16)
    qscale = (D_HEAD ** -0.5) * LOG2E

    def head_of(x, h):
        return pltpu.roll(x, D_HEAD, 1) if h % 2 else x

    for r0 in range(0, tm, th):
        rows = slice(r0, r0 + th)
        u, c0 = r0 // tk, r0 % tk
        h = _rms(x_ref[0, rows], g_ref[...]) * (1.0 + sc_ref[0]) + sh_ref[0]
        proj = jnp.dot(h.astype(BF16), w_ref[...], preferred_element_type=F32)

        ang = pos_ref[0, rows].astype(F32) * invf_ref[...]
        cosv = jnp.cos(ang)
        sinv = jnp.sin(ang)
        sin_signed = jnp.where(first_half, -sinv, sinv)

        def rope(x):
            partner = jnp.where(first_half, pltpu.roll(x, LANES - ROT_DIM // 2, 1),
                                pltpu.roll(x, ROT_DIM // 2, 1))
            return x * cosv + partner * sin_signed

        def slab(base, j):
            return proj[:, base + j * LANES: base + (j + 1) * LANES]

        def put_vt(ref, h, v_slab):
            vt = jnp.where(in_head, head_of(v_slab, h), one_row).T[0:V_ROWS, :]
            ref[0, h, u, :, c0:c0 + th] = vt.astype(BF16)

        for j in range(hd // LANES):
            rq = rope(slab(0, j)) * qscale
            rk = rope(slab(hd, j))
            for h in (2 * j, 2 * j + 1):
                qxd_ref[0, h, rows] = jnp.where(in_head, head_of(rq, h), 0.0).astype(BF16)
                kxd_ref[0, h, rows] = jnp.where(in_head, head_of(rk, h), 0.0).astype(BF16)
                put_vt(vtd_ref, h, slab(2 * hd, j))
        base = 6 * hd
        for j in range(N_IDX_HEADS * D_IDX // LANES):
            iq_ref[0, rows, j * LANES:(j + 1) * LANES] = (rope(slab(base, j)) * (D_IDX ** -0.5)).astype(BF16)
        base += N_IDX_HEADS * D_IDX
        ik_ref[0, rows] = rope(slab(base, 0)).astype(BF16)
        base += LANES

        small = slab(base, 0)
        z = small + bf_ref[...]
        logf = jnp.minimum(z, 0.0) - jnp.log1p(jnp.exp(-jnp.abs(z)))
        logf = jnp.where((lane >= 8) & (lane < 16), logf, 0.0)
        dot = lambda a, b: jnp.dot(a, b.astype(BF16), preferred_element_type=F32)
        l0, l1, l2 = _split3(logf)
        cum = dot(tri, l0) + (dot(tri, l1) + dot(tri, l2)) + carry_ref[...]
        carry_ref[...] = cum[th - 1:th, :]

        aux = jnp.where(lane < N_IDX_HEADS, small * (N_IDX_HEADS ** -0.5), 0.0)
        auxt_ref[0, u, :, c0:c0 + th] = aux.T[0:SUBLANES, :]

        ncum = cum * (-LOG2E)
        for j in range(hd // LANES):
            for h in (2 * j, 2 * j + 1):
                qxf_ref[0, h, rows] = jnp.where(in_head, head_of(slab(3 * hd, j), h) * qscale, ones3).astype(BF16)
                p0, p1, p2 = _split3(ncum[:, 8 + h:9 + h])
                extra = jnp.where(lane == D_HEAD, p0,
                                  jnp.where(lane == D_HEAD + 1, p1, jnp.where(lane == D_HEAD + 2, p2, 0.0)))
                kxf_ref[0, h, rows] = jnp.where(in_head, head_of(slab(4 * hd, j), h), extra).astype(BF16)
                put_vt(vtf_ref, h, slab(5 * hd, j))


def _inproj(x, sh, sc, g, pos, invf, w, bfv, *, hd, tm, tk):
    bsz, s, d = x.shape
    n_cols = w.shape[1]
    nh = hd // D_HEAD
    tok = lambda width: pl.BlockSpec((1, tm, width), lambda b, i: (b, i, 0))
    per_b = pl.BlockSpec((1, 1, d), lambda b, i: (b, 0, 0))
    const = lambda shape: pl.BlockSpec(shape, lambda b, i: (0,) * len(shape))
    headed = jax.ShapeDtypeStruct((bsz, nh, s, LANES), BF16)
    headed_spec = pl.BlockSpec((1, nh, tm, LANES), lambda b, i: (b, 0, i, 0))
    vt = jax.ShapeDtypeStruct((bsz, nh, s // tk, V_ROWS, tk), BF16)
    vt_spec = pl.BlockSpec((1, nh, tm // tk, V_ROWS, tk), lambda b, i: (b, 0, i, 0, 0))
    out_shape = [headed, headed, vt, headed, headed, vt,
                 jax.ShapeDtypeStruct((bsz, s, N_IDX_HEADS * D_IDX), BF16),
                 jax.ShapeDtypeStruct((bsz, s, LANES), BF16),
                 jax.ShapeDtypeStruct((bsz, s // tk, SUBLANES, tk), F32)]
    out_specs = [headed_spec, headed_spec, vt_spec, headed_spec, headed_spec, vt_spec,
                 tok(N_IDX_HEADS * D_IDX), tok(LANES),
                 pl.BlockSpec((1, tm // tk, SUBLANES, tk), lambda b, i: (b, i, 0, 0))]
    return pl.pallas_call(
        functools.partial(_inproj_kernel, hd=hd, tm=tm, tk=tk, th=min(tm, 256)),
        grid=(bsz, s // tm),
        in_specs=[tok(d), per_b, per_b, const((1, d)), tok(1), const((1, LANES)),
                  const((d, n_cols)), const((1, LANES))],
        out_specs=out_specs,
        out_shape=out_shape,
        scratch_shapes=[pltpu.VMEM((1, LANES), F32)],
        compiler_params=_cparams(("arbitrary", "arbitrary")),
        name="inproj",
    )(x, sh, sc, g, pos, invf, w, bfv)


def _flash_t(q, kx_ref, vt_ref, heads, n_plain, t, plain_fn, last_fn, s_refs):
    nh = len(heads)

    def produce(e, kb, fn):
        off = pl.multiple_of(kb * t, t)
        st = fn(lax.dot_general(kx_ref[0, heads[e], pl.ds(off, t), :], q[e], NT_DIMS,
                                preferred_element_type=F32), kb)
        s_refs[e][...] = st
        return jnp.max(st, axis=0, keepdims=True)

    def consume(e, kb, state, bmax):
        m, acc = state
        m_new = jnp.maximum(m, bmax)
        pt = jnp.exp2(s_refs[e][...] - m_new).astype(BF16)
        acc = jnp.exp2(m - m_new) * acc + jnp.dot(vt_ref[0, heads[e], kb], pt, preferred_element_type=F32)
        return m_new, acc

    def body(kb, carry):
        states, bmaxes = carry
        cur = jnp.where(kb == 0, n_plain, kb - 1)
        new_states, new_bmaxes = [], []
        for e in range(nh):
            new_states.append(consume(e, cur, states[e], bmaxes[e]))
            new_bmaxes.append(produce(e, kb, plain_fn))
        return tuple(new_states), tuple(new_bmaxes)

    states = tuple((jnp.full((1, t), -1e30, F32), jnp.zeros((V_ROWS, t), F32)) for _ in heads)
    bmaxes = tuple(produce(e, n_plain, last_fn) for e in range(nh))
    states, bmaxes = lax.fori_loop(0, n_plain, body, (states, bmaxes))
    cur = jnp.maximum(n_plain - 1, 0)
    states = [consume(e, cur, states[e], bmaxes[e]) for e in range(nh)]
    o = jnp.concatenate([acc[0:D_HEAD] / acc[D_HEAD:D_HEAD + 1] for _, acc in states], axis=0)
    return o.T


def _fox_kernel(qx_ref, kx_ref, vt_ref, o_ref, *s_refs, t, hg):
    qi = pl.program_id(2)
    causal = lax.broadcasted_iota(jnp.int32, (t, t), 0) <= lax.broadcasted_iota(jnp.int32, (t, t), 1)
    o = _flash_t([qx_ref[0, e] for e in range(hg)], kx_ref, vt_ref, tuple(range(hg)), qi, t,
                 lambda st, kb: st, lambda st, kb: jnp.where(causal, st, -jnp.inf), s_refs)
    o_ref[0] = o.astype(o_ref.dtype)


def _fox(qx, kx, vt, *, t, hg):
    bsz, nh, s, _ = qx.shape
    nt = s // t
    return pl.pallas_call(
        functools.partial(_fox_kernel, t=t, hg=hg),
        grid=(bsz, nh // hg, nt),
        in_specs=[pl.BlockSpec((1, hg, t, LANES), lambda b, j, i: (b, j, i, 0)),
                  pl.BlockSpec((1, hg, s, LANES), lambda b, j, i: (b, j, 0, 0)),
                  pl.BlockSpec((1, hg, nt, V_ROWS, t), lambda b, j, i: (b, j, 0, 0, 0))],
        out_specs=pl.BlockSpec((1, t, hg * D_HEAD), lambda b, j, i: (b, i, j)),
        out_shape=jax.ShapeDtypeStruct((bsz, s, nh * D_HEAD), BF16),
        scratch_shapes=[pltpu.VMEM((t, t), F32)] * hg,
        compiler_params=_cparams(("arbitrary", "arbitrary", "arbitrary")),
        name="fox_attn",
    )(qx, kx, vt)


def _dsa_kernel(qx_ref, iq_ref, auxt_ref, kx_ref, vt_ref, ik_ref, o_ref, sc_ref, thr_ref, og_ref, *s_refs,
                t, n_heads, k_top):
    qi = pl.program_id(1)
    n_kb = qi + 1
    kidx = lax.broadcasted_iota(jnp.int32, (t, t), 0)
    qidx = lax.broadcasted_iota(jnp.int32, (t, t), 1)
    lane = lax.broadcasted_iota(jnp.int32, (t, LANES), 1)
    iw = auxt_ref[0, 0]

    iqm = []
    for h in range(N_IDX_HEADS):
        iq2 = iq_ref[0, :, (h // 2) * LANES:(h // 2 + 1) * LANES]
        keep = (lane < D_IDX) if h % 2 == 0 else (lane >= D_IDX)
        iqm.append(iq2 * jnp.where(keep, 1.0, 0.0).astype(BF16))

    def score_block(kb, masked, lohi):
        off = pl.multiple_of(kb * t, t)
        ik2 = ik_ref[0, pl.ds(off, t), :]
        sc = jnp.zeros((t, t), F32)
        for h in range(N_IDX_HEADS):
            d = lax.dot_general(ik2, iqm[h], NT_DIMS, preferred_element_type=F32)
            sc = sc + jnp.maximum(d, 0.0) * iw[h:h + 1, :]
        sc_min = sc
        if masked:
            causal = kidx <= qidx
            sc_min = jnp.where(causal, sc, jnp.inf)
            sc = jnp.where(causal, sc, -jnp.inf)
        sc_ref[kb] = sc
        return (jnp.minimum(lohi[0], jnp.min(sc_min, axis=0, keepdims=True)),
                jnp.maximum(lohi[1], jnp.max(sc, axis=0, keepdims=True)))

    lohi = (jnp.full((1, t), jnp.inf, F32), jnp.full((1, t), -jnp.inf, F32))
    lohi = lax.fori_loop(0, qi, lambda kb, c: score_block(kb, False, c), lohi)
    smin, smax = score_block(qi, True, lohi)

    def count(cmp_fn):
        def body(kb, tot):
            return tot + jnp.sum(jnp.where(cmp_fn(sc_ref[kb]), 1.0, 0.0), axis=0, keepdims=True)
        return lax.fori_loop(0, n_kb, body, jnp.zeros((1, t), F32))

    kf = float(k_top)
    n_causal = (qi * t + 1 + lax.broadcasted_iota(jnp.int32, (1, t), 1)).astype(F32)
    c0_ge = count(lambda x: x >= 0.0)
    c0_gt = count(lambda x: x > 0.0)
    few = n_causal <= kf
    positive = c0_gt >= kf
    at_zero = jnp.logical_and(jnp.logical_not(positive), c0_ge >= kf)
    hi_pos = smax + jnp.maximum(jnp.abs(smax) * 1e-6, 1e-30)
    lo = jnp.where(few, -F32_MAX, jnp.where(at_zero, 0.0, jnp.where(positive, F32_TINY, smin)))
    c_lo = jnp.where(few, n_causal, jnp.where(at_zero, c0_ge, jnp.where(positive, c0_gt, n_causal)))
    hi = jnp.where(positive, hi_pos, -F32_TINY)
    c_hi = jnp.where(positive, 0.0, c0_ge)
    done = jnp.where(jnp.logical_or(jnp.logical_or(few, at_zero), c_lo <= kf), 1.0, 0.0)

    def narrow(carry):
        it, _, lo, hi, c_lo, c_hi, done, bisect = carry
        mid = 0.5 * lo + 0.5 * hi
        stuck = jnp.logical_or(mid <= lo, mid >= hi)
        a, b = jnp.log(jnp.maximum(c_lo, 0.5)), jnp.log(jnp.maximum(c_hi, 0.5))
        frac = jnp.clip((a - np.log(kf)) / jnp.maximum(a - b, 1e-6), 0.02, 0.98)
        interp = lo + (hi - lo) * frac
        use_mid = jnp.logical_or(bisect > 0.5, jnp.logical_or(interp <= lo, interp >= hi))
        cand = jnp.where(use_mid, mid, interp)
        c = count(lambda x: x >= cand)
        active = jnp.logical_and(done < 0.5, jnp.logical_not(stuck))
        up = jnp.logical_and(active, c >= kf)
        down = jnp.logical_and(active, c < kf)
        inside = c_lo - c_hi
        lo, c_lo = jnp.where(up, cand, lo), jnp.where(up, c, c_lo)
        hi, c_hi = jnp.where(down, cand, hi), jnp.where(down, c, c_hi)
        done = jnp.where(jnp.logical_or(stuck, c_lo <= kf), 1.0, done)
        slow = jnp.logical_and(jnp.logical_not(use_mid), c_lo - c_hi > 0.5 * inside)
        return it + 1, jnp.min(done), lo, hi, c_lo, c_hi, done, jnp.where(slow, 1.0, 0.0)

    carry = (jnp.int32(0), jnp.min(done), lo, hi, c_lo, c_hi, done, jnp.zeros((1, t), F32))
    carry = lax.while_loop(lambda c: jnp.logical_and(c[0] < MAX_NARROW_STEPS, c[1] < 0.5), narrow, carry)
    _, all_done, lo, _, c_lo, _, _, _ = carry
    thr_ref[0:1, :] = lo
    thr_ref[1:2, :] = c_lo

    @pl.when(all_done < 0.5)
    def _():
        def to_float(key):
            return pltpu.bitcast(key ^ ((key >> 31) & 0x7FFFFFFF), F32)

        def search(it, u):
            bit = lax.shift_left(jnp.int32(1), 31 - it)
            cand = to_float((u | bit) ^ INT_MIN)
            return jnp.where(count(lambda x: x >= cand) >= kf, u | bit, u)

        u = lax.fori_loop(0, 32, search, jnp.zeros((1, t), jnp.int32))
        thr_bits = to_float(jnp.maximum(u ^ INT_MIN, KEY_NEG_INF + 1))
        thr_ref[0:1, :] = thr_bits
        thr_ref[1:2, :] = count(lambda x: x >= thr_bits)

    thr = thr_ref[0:1, :]
    has_ties = jnp.max(thr_ref[1:2, :]) > kf

    @pl.when(jnp.logical_not(has_ties))
    def _():
        def body(kb, _):
            sc_ref[kb] = jnp.where(sc_ref[kb] >= thr, 0.0, -jnp.inf)
            return 0
        lax.fori_loop(0, n_kb, body, 0)

    @pl.when(has_ties)
    def _():
        need = kf - count(lambda x: x > thr)
        earlier = jnp.where(qidx < kidx, 1.0, 0.0).astype(BF16)

        def body(kb, run):
            x = sc_ref[kb]
            eq = jnp.where(x == thr, 1.0, 0.0)
            rank = jnp.dot(earlier, eq.astype(BF16), preferred_element_type=F32) + run
            sel = jnp.where(x > thr, 1.0, jnp.where(rank < need, eq, 0.0))
            sc_ref[kb] = jnp.where(sel > 0.5, 0.0, -jnp.inf)
            return run + jnp.sum(eq, axis=0, keepdims=True)
        lax.fori_loop(0, n_kb, body, jnp.zeros((1, t), F32))

    add_mask = lambda st, kb: st + sc_ref[kb]
    hg = len(s_refs)

    def head_group(j, _):
        heads = [j * hg + e for e in range(hg)]
        o = _flash_t([qx_ref[0, h] for h in heads], kx_ref, vt_ref, heads, qi, t, add_mask, add_mask, s_refs)
        og_ref[j] = o.astype(og_ref.dtype)
        return 0

    lax.fori_loop(0, n_heads // hg, head_group, 0)
    for j in range(n_heads // hg):
        o_ref[0, :, j * hg * D_HEAD:(j + 1) * hg * D_HEAD] = og_ref[j]


def _dsa(qx, iq, auxt, kx, vt, ik, *, t, k_top, hg):
    bsz, nh, s, _ = qx.shape
    nt = s // t
    return pl.pallas_call(
        functools.partial(_dsa_kernel, t=t, n_heads=nh, k_top=k_top),
        grid=(bsz, nt),
        in_specs=[pl.BlockSpec((1, nh, t, LANES), lambda b, i: (b, 0, i, 0)),
                  pl.BlockSpec((1, t, N_IDX_HEADS * D_IDX), lambda b, i: (b, i, 0)),
                  pl.BlockSpec((1, 1, SUBLANES, t), lambda b, i: (b, i, 0, 0)),
                  pl.BlockSpec((1, nh, s, LANES), lambda b, i: (b, 0, 0, 0), pipeline_mode=pl.Buffered(1)),
                  pl.BlockSpec((1, nh, nt, V_ROWS, t), lambda b, i: (b, 0, 0, 0, 0),
                               pipeline_mode=pl.Buffered(1)),
                  pl.BlockSpec((1, s, LANES), lambda b, i: (b, 0, 0), pipeline_mode=pl.Buffered(1))],
        out_specs=pl.BlockSpec((1, t, nh * D_HEAD), lambda b, i: (b, i, 0)),
        out_shape=jax.ShapeDtypeStruct((bsz, s, nh * D_HEAD), BF16),
        scratch_shapes=[pltpu.VMEM((nt, t, t), F32), pltpu.VMEM((SUBLANES, t), F32),
                        pltpu.VMEM((nh // hg, t, hg * D_HEAD), BF16)]
        + [pltpu.VMEM((t, t), F32)] * hg,
        compiler_params=_cparams(("arbitrary", "arbitrary")),
        name="dsa_attn",
    )(qx, iq, auxt, kx, vt, ik)


def _ffn_kernel(x_ref, yd_ref, yf_ref, dg_ref, xg_ref, g1_ref, sh_ref, sc_ref, g2_ref, ng_ref, fg_ref,
                wo_ref, wg_ref, wu_ref, wd_ref, o_ref, *, hd):
    dot = functools.partial(jnp.dot, preferred_element_type=F32)
    yd = _rms(yd_ref[0].astype(F32), dg_ref[...]).astype(BF16)
    yf = _rms(yf_ref[0].astype(F32), xg_ref[...]).astype(BF16)
    x1 = x_ref[0] + g1_ref[0] * (dot(yd, wo_ref[0:hd, :]) + dot(yf, wo_ref[hd:, :]))
    h = (_rms(x1, ng_ref[...]) * (1.0 + sc_ref[0]) + sh_ref[0]).astype(BF16)
    gate = dot(h, wg_ref[...])
    act = gate * jax.nn.sigmoid(gate) * dot(h, wu_ref[...])
    x2 = x1 + g2_ref[0] * dot(act.astype(BF16), wd_ref[...])
    o_ref[0] = _rms(x2, fg_ref[...])


def _ffn(x, yd, yf, dg, xg, g1, sh, sc, g2, ng, fg, wo, wg, wu, wd, *, tm):
    bsz, s, d = x.shape
    hd = yd.shape[-1]
    tok = lambda width: pl.BlockSpec((1, tm, width), lambda b, i: (b, i, 0))
    per_b = pl.BlockSpec((1, 1, d), lambda b, i: (b, 0, 0))
    const = lambda shape: pl.BlockSpec(shape, lambda b, i: (0, 0), pipeline_mode=pl.Buffered(1))
    return pl.pallas_call(
        functools.partial(_ffn_kernel, hd=hd),
        grid=(bsz, s // tm),
        in_specs=[tok(d), tok(hd), tok(hd), const((1, hd)), const((1, hd)), per_b, per_b, per_b, per_b,
                  const((1, d)), const((1, d)),
                  const(wo.shape), const(wg.shape), const(wu.shape), const(wd.shape)],
        out_specs=tok(d),
        out_shape=jax.ShapeDtypeStruct((bsz, s, d), F32),
        compiler_params=_cparams(("arbitrary", "arbitrary")),
        name="outproj_ffn",
    )(x, yd, yf, dg, xg, g1, sh, sc, g2, ng, fg, wo, wg, wu, wd)


def _pick_tile(n, target):
    t = min(n, target)
    while n % t:
        t //= 2
    return t


def kernel(x, c, positions, ada_w, ada_b, norm_attn_g, w_in, b_forget, dsa_norm_g, fox_norm_g,
           w_out, norm_ffn_g, w_gate, w_up, w_down, final_norm_g):
    bsz, s, d = x.shape
    depth = ada_w.shape[0]
    n_fox = b_forget.shape[1]
    hd = dsa_norm_g.shape[1]
    assert depth == 1, "the fused final RMSNorm assumes a single layer"
    assert n_fox == hd // D_HEAD == 8 and hd % LANES == 0
    k_top = min(TOP_K_MAX, s // 4)
    t_attn = _pick_tile(s, 512)
    tm_in = _pick_tile(s, 512)
    tm_ffn = _pick_tile(s, 512)

    sizes = (hd, hd, hd, N_IDX_HEADS * D_IDX, D_IDX, N_IDX_HEADS, hd, hd, hd, n_fox)
    pts = np.cumsum((0,) + sizes)
    wl = w_in[0]
    seg = lambda k: wl[:, pts[k]:pts[k + 1]]
    small = jnp.zeros((d, LANES), F32)
    small = small.at[:, 0:N_IDX_HEADS].set(seg(5)).at[:, 8:8 + n_fox].set(seg(9))
    w_cat = jnp.concatenate([seg(0), seg(1), seg(2), seg(6), seg(7), seg(8),
                             seg(3), seg(4), seg(4), small], axis=1).astype(BF16)
    bfv = jnp.zeros((1, LANES), F32).at[0, 8:8 + n_fox].set(b_forget[0])

    lane = jnp.arange(LANES)
    inv_freq = ROPE_THETA ** (-jnp.arange(0, ROT_DIM, 2, dtype=F32) / ROT_DIM)
    invf = jnp.where((lane % D_HEAD) < ROT_DIM, inv_freq[lane % (ROT_DIM // 2)], 0.0).reshape(1, LANES)
    pos = positions.reshape(bsz, s, 1)

    mod = _modulation(c, ada_w[0], ada_b[0])
    sh1, sc1, g1, sh2, sc2, g2 = [m.reshape(bsz, 1, d) for m in jnp.split(mod, N_MOD, axis=-1)]

    qxd, kxd, vtd, qxf, kxf, vtf, iq, ik, auxt = _inproj(
        x, sh1, sc1, norm_attn_g[0].reshape(1, d), pos, invf, w_cat, bfv, hd=hd, tm=tm_in, tk=t_attn)

    yf = _fox(qxf, kxf, vtf, t=t_attn, hg=8)
    yd = _dsa(qxd, iq, auxt, kxd, vtd, ik, t=t_attn, k_top=k_top, hg=8)

    return _ffn(x, yd, yf, dsa_norm_g[0].reshape(1, hd), fox_norm_g[0].reshape(1, hd),
                g1, sh2, sc2, g2, norm_ffn_g[0].reshape(1, d), final_norm_g.reshape(1, d),
                w_out[0].astype(BF16), w_gate[0].astype(BF16), w_up[0].astype(BF16),
                w_down[0].astype(BF16), tm=tm_ffn)
```

```python
import functools

import numpy as np
import jax
import jax.numpy as jnp
from jax import lax
from jax.experimental import pallas as pl
from jax.experimental.pallas import tpu as pltpu

F32 = jnp.float32
BF16 = jnp.bfloat16

D_HEAD = 64
N_IDX_HEADS = 4
D_IDX = 64
TOP_K_MAX = 256
ROPE_THETA = 500000.0
ROT_DIM = D_HEAD // 4
N_MOD = 6
RMS_EPS = 1e-6

LOG2E = 1.4426950408889634
V_ROWS = 80
LANES = 128
SUBLANES = 8
VMEM_LIMIT = 56 * 1024 * 1024

F32_MAX = float(np.finfo(np.float32).max)
F32_TINY = float(np.finfo(np.float32).tiny)
MAX_NARROW_STEPS = 48
INT_MIN = -(2 ** 31)
KEY_NEG_INF = int(np.int32(np.uint32(0xFF800000) ^ np.uint32(0x7FFFFFFF)))

NT_DIMS = (((1,), (1,)), ((), ()))


def _cparams(sem):
    return pltpu.CompilerParams(dimension_semantics=sem, vmem_limit_bytes=VMEM_LIMIT)


def _split3(a):
    a0 = a.astype(BF16).astype(F32)
    a1 = (a - a0).astype(BF16).astype(F32)
    return a0, a1, a - a0 - a1


def _rms(x, g):
    return x * lax.rsqrt(jnp.mean(x * x, axis=-1, keepdims=True) + RMS_EPS) * g


def _mod_kernel(c_ref, w_ref, b_ref, o_ref):
    c = c_ref[...]
    a0, a1, _ = _split3(c * jax.nn.sigmoid(c))
    w0, w1, _ = _split3(w_ref[...])
    dot = lambda a, b: jnp.dot(a.astype(BF16), b.astype(BF16), preferred_element_type=F32)
    o_ref[...] = dot(a0, w0) + (dot(a0, w1) + dot(a1, w0)) + b_ref[...]


def _modulation(c, w, b):
    bsz, d = c.shape
    n = w.shape[1]
    tn = 1024
    return pl.pallas_call(
        _mod_kernel,
        grid=(n // tn,),
        in_specs=[pl.BlockSpec((bsz, d), lambda j: (0, 0)),
                  pl.BlockSpec((d, tn), lambda j: (0, j)),
                  pl.BlockSpec((1, tn), lambda j: (0, j))],
        out_specs=pl.BlockSpec((bsz, tn), lambda j: (0, j)),
        out_shape=jax.ShapeDtypeStruct((bsz, n), F32),
        compiler_params=_cparams(("arbitrary",)),
        name="adaln_mod",
    )(c, w, b.reshape(1, n))


def _inproj_kernel(x_ref, sh_ref, sc_ref, g_ref, pos_ref, invf_ref, w_ref, bf_ref,
                   qxd_ref, kxd_ref, vtd_ref, qxf_ref, kxf_ref, vtf_ref, iq_ref, ik_ref, auxt_ref,
                   carry_ref, *, hd, tm, tk, th):
    i = pl.program_id(1)

    @pl.when(i == 0)
    def _():
        carry_ref[...] = jnp.zeros_like(carry_ref)

    lane = lax.broadcasted_iota(jnp.int32, (th, LANES), 1)
    first_half = (lane & (D_HEAD - 1)) < (ROT_DIM // 2)
    in_head = lane < D_HEAD
    one_row = jnp.where(lane == D_HEAD, 1.0, 0.0)
    ones3 = jnp.where(lane < D_HEAD + 3, 1.0, 0.0)
    r = lax.broadcasted_iota(jnp.int32, (th, th), 0)
    c = lax.broadcasted_iota(jnp.int32, (th, th), 1)
    tri = jnp.where(c <= r, 1.0, 0.0).astype(BF16)
    qscale = (D_HEAD ** -0.5) * LOG2E

    def head_of(x, h):
        return pltpu.roll(x, D_HEAD, 1) if h % 2 else x

    for r0 in range(0, tm, th):
        rows = slice(r0, r0 + th)
        u, c0 = r0 // tk, r0 % tk
        h = _rms(x_ref[0, rows], g_ref[...]) * (1.0 + sc_ref[0]) + sh_ref[0]
        proj = jnp.dot(h.astype(BF16), w_ref[...], preferred_element_type=F32)

        ang = pos_ref[0, rows].astype(F32) * invf_ref[...]
        cosv = jnp.cos(ang)
        sinv = jnp.sin(ang)
        sin_signed = jnp.where(first_half, -sinv, sinv)

        def rope(x):
            partner = jnp.where(first_half, pltpu.roll(x, LANES - ROT_DIM // 2, 1),
                                pltpu.roll(x, ROT_DIM // 2, 1))
            return x * cosv + partner * sin_signed

        def slab(base, j):
            return proj[:, base + j * LANES: base + (j + 1) * LANES]

        def put_vt(ref, h, v_slab):
            vt = jnp.where(in_head, head_of(v_slab, h), one_row).T[0:V_ROWS, :]
            ref[0, h, u, :, c0:c0 + th] = vt.astype(BF16)

        for j in range(hd // LANES):
            rq = rope(slab(0, j)) * qscale
            rk = rope(slab(hd, j))
            for h in (2 * j, 2 * j + 1):
                qxd_ref[0, h, rows] = jnp.where(in_head, head_of(rq, h), 0.0).astype(BF16)
                kxd_ref[0, h, rows] = jnp.where(in_head, head_of(rk, h), 0.0).astype(BF16)
                put_vt(vtd_ref, h, slab(2 * hd, j))
        base = 6 * hd
        for j in range(N_IDX_HEADS * D_IDX // LANES):
            iq_ref[0, rows, j * LANES:(j + 1) * LANES] = (rope(slab(base, j)) * (D_IDX ** -0.5)).astype(BF16)
        base += N_IDX_HEADS * D_IDX
        ik_ref[0, rows] = rope(slab(base, 0)).astype(BF16)
        base += LANES

        small = slab(base, 0)
        z = small + bf_ref[...]
        logf = jnp.minimum(z, 0.0) - jnp.log1p(jnp.exp(-jnp.abs(z)))
        logf = jnp.where((lane >= 8) & (lane < 16), logf, 0.0)
        dot = lambda a, b: jnp.dot(a, b.astype(BF16), preferred_element_type=F32)
        l0, l1, l2 = _split3(logf)
        cum = dot(tri, l0) + (dot(tri, l1) + dot(tri, l2)) + carry_ref[...]
        carry_ref[...] = cum[th - 1:th, :]

        aux = jnp.where(lane < N_IDX_HEADS, small * (N_IDX_HEADS ** -0.5), 0.0)
        auxt_ref[0, u, :, c0:c0 + th] = aux.T[0:SUBLANES, :]

        ncum = cum * (-LOG2E)
        for j in range(hd // LANES):
            for h in (2 * j, 2 * j + 1):
                qxf_ref[0, h, rows] = jnp.where(in_head, head_of(slab(3 * hd, j), h) * qscale, ones3).astype(BF16)
                p0, p1, p2 = _split3(ncum[:, 8 + h:9 + h])
                extra = jnp.where(lane == D_HEAD, p0,
                                  jnp.where(lane == D_HEAD + 1, p1, jnp.where(lane == D_HEAD + 2, p2, 0.0)))
                kxf_ref[0, h, rows] = jnp.where(in_head, head_of(slab(4 * hd, j), h), extra).astype(BF16)
                put_vt(vtf_ref, h, slab(5 * hd, j))


def _inproj(x, sh, sc, g, pos, invf, w, bfv, *, hd, tm, tk):
    bsz, s, d = x.shape
    n_cols = w.shape[1]
    nh = hd // D_HEAD
    tok = lambda width: pl.BlockSpec((1, tm, width), lambda b, i: (b, i, 0))
    per_b = pl.BlockSpec((1, 1, d), lambda b, i: (b, 0, 0))
    const = lambda shape: pl.BlockSpec(shape, lambda b, i: (0,) * len(shape))
    headed = jax.ShapeDtypeStruct((bsz, nh, s, LANES), BF16)
    headed_spec = pl.BlockSpec((1, nh, tm, LANES), lambda b, i: (b, 0, i, 0))
    vt = jax.ShapeDtypeStruct((bsz, nh, s // tk, V_ROWS, tk), BF16)
    vt_spec = pl.BlockSpec((1, nh, tm // tk, V_ROWS, tk), lambda b, i: (b, 0, i, 0, 0))
    out_shape = [headed, headed, vt, headed, headed, vt,
                 jax.ShapeDtypeStruct((bsz, s, N_IDX_HEADS * D_IDX), BF16),
                 jax.ShapeDtypeStruct((bsz, s, LANES), BF16),
                 jax.ShapeDtypeStruct((bsz, s // tk, SUBLANES, tk), F32)]
    out_specs = [headed_spec, headed_spec, vt_spec, headed_spec, headed_spec, vt_spec,
                 tok(N_IDX_HEADS * D_IDX), tok(LANES),
                 pl.BlockSpec((1, tm // tk, SUBLANES, tk), lambda b, i: (b, i, 0, 0))]
    return pl.pallas_call(
        functools.partial(_inproj_kernel, hd=hd, tm=tm, tk=tk, th=min(tm, 256)),
        grid=(bsz, s // tm),
        in_specs=[tok(d), per_b, per_b, const((1, d)), tok(1), const((1, LANES)),
                  const((d, n_cols)), const((1, LANES))],
        out_specs=out_specs,
        out_shape=out_shape,
        scratch_shapes=[pltpu.VMEM((1, LANES), F32)],
        compiler_params=_cparams(("arbitrary", "arbitrary")),
        name="inproj",
    )(x, sh, sc, g, pos, invf, w, bfv)


def _flash_t(q, kx_ref, vt_ref, heads, n_plain, t, plain_fn, last_fn, s_refs):
    nh = len(heads)

    def produce(e, kb, fn):
        off = pl.multiple_of(kb * t, t)
        st = fn(lax.dot_general(kx_ref[0, heads[e], pl.ds(off, t), :], q[e], NT_DIMS,
                                preferred_element_type=F32), kb)
        s_refs[e][...] = st
        return jnp.max(st, axis=0, keepdims=True)

    def consume(e, kb, state, bmax):
        m, acc = state
        m_new = jnp.maximum(m, bmax)
        pt = jnp.exp2(s_refs[e][...] - m_new).astype(BF16)
        acc = jnp.exp2(m - m_new) * acc + jnp.dot(vt_ref[0, heads[e], kb], pt, preferred_element_type=F32)
        return m_new, acc

    def body(kb, carry):
        states, bmaxes = carry
        cur = jnp.where(kb == 0, n_plain, kb - 1)
        new_states, new_bmaxes = [], []
        for e in range(nh):
            new_states.append(consume(e, cur, states[e], bmaxes[e]))
            new_bmaxes.append(produce(e, kb, plain_fn))
        return tuple(new_states), tuple(new_bmaxes)

    states = tuple((jnp.full((1, t), -1e30, F32), jnp.zeros((V_ROWS, t), F32)) for _ in heads)
    bmaxes = tuple(produce(e, n_plain, last_fn) for e in range(nh))
    states, bmaxes = lax.fori_loop(0, n_plain, body, (states, bmaxes))
    cur = jnp.maximum(n_plain - 1, 0)
    states = [consume(e, cur, states[e], bmaxes[e]) for e in range(nh)]
    o = jnp.concatenate([acc[0:D_HEAD] / acc[D_HEAD:D_HEAD + 1] for _, acc in states], axis=0)
    return o.T


def _fox_kernel(qx_ref, kx_ref, vt_ref, o_ref, *s_refs, t, hg):
    qi = pl.program_id(2)
    causal = lax.broadcasted_iota(jnp.int32, (t, t), 0) <= lax.broadcasted_iota(jnp.int32, (t, t), 1)
    o = _flash_t([qx_ref[0, e] for e in range(hg)], kx_ref, vt_ref, tuple(range(hg)), qi, t,
                 lambda st, kb: st, lambda st, kb: jnp.where(causal, st, -jnp.inf), s_refs)
    o_ref[0] = o.astype(o_ref.dtype)


def _fox(qx, kx, vt, *, t, hg):
    bsz, nh, s, _ = qx.shape
    nt = s // t
    return pl.pallas_call(
        functools.partial(_fox_kernel, t=t, hg=hg),
        grid=(bsz, nh // hg, nt),
        in_specs=[pl.BlockSpec((1, hg, t, LANES), lambda b, j, i: (b, j, i, 0)),
                  pl.BlockSpec((1, hg, s, LANES), lambda b, j, i: (b, j, 0, 0)),
                  pl.BlockSpec((1, hg, nt, V_ROWS, t), lambda b, j, i: (b, j, 0, 0, 0))],
        out_specs=pl.BlockSpec((1, t, hg * D_HEAD), lambda b, j, i: (b, i, j)),
        out_shape=jax.ShapeDtypeStruct((bsz, s, nh * D_HEAD), BF16),
        scratch_shapes=[pltpu.VMEM((t, t), F32)] * hg,
        compiler_params=_cparams(("arbitrary", "arbitrary", "arbitrary")),
        name="fox_attn",
    )(qx, kx, vt)


def _dsa_kernel(qx_ref, iq_ref, auxt_ref, kx_ref, vt_ref, ik_ref, o_ref, sc_ref, thr_ref, og_ref, *s_refs,
                t, n_heads, k_top):
    qi = pl.program_id(1)
    n_kb = qi + 1
    kidx = lax.broadcasted_iota(jnp.int32, (t, t), 0)
    qidx = lax.broadcasted_iota(jnp.int32, (t, t), 1)
    lane = lax.broadcasted_iota(jnp.int32, (t, LANES), 1)
    iw = auxt_ref[0, 0]

    iqm = []
    for h in range(N_IDX_HEADS):
        iq2 = iq_ref[0, :, (h // 2) * LANES:(h // 2 + 1) * LANES]
        keep = (lane < D_IDX) if h % 2 == 0 else (lane >= D_IDX)
        iqm.append(iq2 * jnp.where(keep, 1.0, 0.0).astype(BF16))

    def score_block(kb, masked, stats):
        off = pl.multiple_of(kb * t, t)
        ik2 = ik_ref[0, pl.ds(off, t), :]
        sc = jnp.zeros((t, t), F32)
        for h in range(N_IDX_HEADS):
            d = lax.dot_general(ik2, iqm[h], NT_DIMS, preferred_element_type=F32)
            sc = sc + jnp.maximum(d, 0.0) * iw[h:h + 1, :]
        sc_min = sc
        if masked:
            causal = kidx <= qidx
            sc_min = jnp.where(causal, sc, jnp.inf)
            sc = jnp.where(causal, sc, -jnp.inf)
        sc_ref[kb] = sc
        col = lambda x, op: op(x, axis=0, keepdims=True)
        return (jnp.minimum(stats[0], col(sc_min, jnp.min)), jnp.maximum(stats[1], col(sc, jnp.max)),
                stats[2] + col(jnp.where(sc >= 0.0, 1.0, 0.0), jnp.sum),
                stats[3] + col(jnp.where(sc > 0.0, 1.0, 0.0), jnp.sum))

    zeros = jnp.zeros((1, t), F32)
    stats = (jnp.full((1, t), jnp.inf, F32), jnp.full((1, t), -jnp.inf, F32), zeros, zeros)
    stats = lax.fori_loop(0, qi, lambda kb, c: score_block(kb, False, c), stats)
    smin, smax, c0_ge, c0_gt = score_block(qi, True, stats)

    def count(cmp_fn):
        def body(kb, tot):
            return tot + jnp.sum(jnp.where(cmp_fn(sc_ref[kb]), 1.0, 0.0), axis=0, keepdims=True)
        return lax.fori_loop(0, n_kb, body, jnp.zeros((1, t), F32))

    kf = float(k_top)
    n_causal = (qi * t + 1 + lax.broadcasted_iota(jnp.int32, (1, t), 1)).astype(F32)
    few = n_causal <= kf
    positive = c0_gt >= kf
    at_zero = jnp.logical_and(jnp.logical_not(positive), c0_ge >= kf)
    hi_pos = smax + jnp.maximum(jnp.abs(smax) * 1e-6, 1e-30)
    lo = jnp.where(few, -F32_MAX, jnp.where(at_zero, 0.0, jnp.where(positive, F32_TINY, smin)))
    c_lo = jnp.where(few, n_causal, jnp.where(at_zero, c0_ge, jnp.where(positive, c0_gt, n_causal)))
    hi = jnp.where(positive, hi_pos, -F32_TINY)
    c_hi = jnp.where(positive, 0.0, c0_ge)
    done = jnp.where(jnp.logical_or(jnp.logical_or(few, at_zero), c_lo <= kf), 1.0, 0.0)

    log_k = float(np.log(kf))
    f_of = lambda cnt: jnp.log(jnp.maximum(cnt, 0.5)) - log_k

    def narrow(carry):
        it, _, lo, hi, c_lo, c_hi, f_lo, f_hi, side, done = carry
        mid = 0.5 * lo + 0.5 * hi
        stuck = jnp.logical_or(mid <= lo, mid >= hi)
        interp = lo + (hi - lo) * jnp.clip(f_lo / jnp.maximum(f_lo - f_hi, 1e-6), 0.01, 0.99)
        cand = jnp.where(jnp.logical_or(interp <= lo, interp >= hi), mid, interp)
        c = count(lambda x: x >= cand)
        f_c = f_of(c)
        active = jnp.logical_and(done < 0.5, jnp.logical_not(stuck))
        up = jnp.logical_and(active, c >= kf)
        down = jnp.logical_and(active, c < kf)
        f_hi = jnp.where(jnp.logical_and(up, side > 0.5), 0.5 * f_hi, f_hi)
        f_lo = jnp.where(jnp.logical_and(down, side < -0.5), 0.5 * f_lo, f_lo)
        lo, c_lo, f_lo = jnp.where(up, cand, lo), jnp.where(up, c, c_lo), jnp.where(up, f_c, f_lo)
        hi, c_hi, f_hi = jnp.where(down, cand, hi), jnp.where(down, c, c_hi), jnp.where(down, f_c, f_hi)
        side = jnp.where(up, 1.0, jnp.where(down, -1.0, side))
        done = jnp.where(jnp.logical_or(stuck, jnp.logical_or(c_lo <= kf, c_lo - c_hi <= 2.0)), 1.0, done)
        return it + 1, jnp.min(done), lo, hi, c_lo, c_hi, f_lo, f_hi, side, done

    carry = (jnp.int32(0), jnp.min(done), lo, hi, c_lo, c_hi, f_of(c_lo), f_of(c_hi),
             jnp.zeros((1, t), F32), done)
    carry = lax.while_loop(lambda c: jnp.logical_and(c[0] < MAX_NARROW_STEPS, c[1] < 0.5), narrow, carry)
    _, all_done, lo, hi, c_lo, _, _, _, _, _ = carry
    thr_ref[0:1, :] = lo
    thr_ref[1:2, :] = c_lo

    extract = jnp.logical_and(c_lo > kf, jnp.logical_not(jnp.logical_or(few, at_zero)))

    @pl.when(jnp.logical_and(all_done > 0.5, jnp.max(jnp.where(extract, 1.0, 0.0)) > 0.5))
    def _():
        def body(kb, best):
            x = sc_ref[kb]
            return jnp.maximum(best, jnp.max(jnp.where(x < hi, x, -jnp.inf), axis=0, keepdims=True))
        below = lax.fori_loop(0, n_kb, body, jnp.full((1, t), -jnp.inf, F32))
        kth = jnp.where(extract, below, lo)
        thr_ref[0:1, :] = kth
        thr_ref[1:2, :] = count(lambda x: x >= kth)

    @pl.when(all_done < 0.5)
    def _():
        def to_float(key):
            return pltpu.bitcast(key ^ ((key >> 31) & 0x7FFFFFFF), F32)

        def search(it, u):
            bit = lax.shift_left(jnp.int32(1), 31 - it)
            cand = to_float((u | bit) ^ INT_MIN)
            return jnp.where(count(lambda x: x >= cand) >= kf, u | bit, u)

        u = lax.fori_loop(0, 32, search, jnp.zeros((1, t), jnp.int32))
        thr_bits = to_float(jnp.maximum(u ^ INT_MIN, KEY_NEG_INF + 1))
        thr_ref[0:1, :] = thr_bits
        thr_ref[1:2, :] = count(lambda x: x >= thr_bits)

    thr = thr_ref[0:1, :]
    has_ties = jnp.max(thr_ref[1:2, :]) > kf

    @pl.when(jnp.logical_not(has_ties))
    def _():
        def body(kb, _):
            sc_ref[kb] = jnp.where(sc_ref[kb] >= thr, 0.0, -jnp.inf)
            return 0
        lax.fori_loop(0, n_kb, body, 0)

    @pl.when(has_ties)
    def _():
        need = kf - count(lambda x: x > thr)
        earlier = jnp.where(qidx < kidx, 1.0, 0.0).astype(BF16)

        def body(kb, run):
            x = sc_ref[kb]
            eq = jnp.where(x == thr, 1.0, 0.0).astype(BF16)
            room = need - run
            rank = jnp.dot(earlier, eq, preferred_element_type=F32)
            tie_mask = jnp.where(rank < room, 0.0, -jnp.inf)
            sc_ref[kb] = jnp.where(x > thr, 0.0, jnp.where(x == thr, tie_mask, -jnp.inf))
            return run + rank[t - 1:t, :] + eq[t - 1:t, :].astype(F32)
        lax.fori_loop(0, n_kb, body, jnp.zeros((1, t), F32))

    add_mask = lambda st, kb: st + sc_ref[kb]
    hg = len(s_refs)

    def head_group(j, _):
        heads = [j * hg + e for e in range(hg)]
        o = _flash_t([qx_ref[0, h] for h in heads], kx_ref, vt_ref, heads, qi, t, add_mask, add_mask, s_refs)
        og_ref[j] = o.astype(og_ref.dtype)
        return 0

    lax.fori_loop(0, n_heads // hg, head_group, 0)
    for j in range(n_heads // hg):
        o_ref[0, :, j * hg * D_HEAD:(j + 1) * hg * D_HEAD] = og_ref[j]


def _dsa(qx, iq, auxt, kx, vt, ik, *, t, k_top, hg):
    bsz, nh, s, _ = qx.shape
    nt = s // t
    return pl.pallas_call(
        functools.partial(_dsa_kernel, t=t, n_heads=nh, k_top=k_top),
        grid=(bsz, nt),
        in_specs=[pl.BlockSpec((1, nh, t, LANES), lambda b, i: (b, 0, i, 0)),
                  pl.BlockSpec((1, t, N_IDX_HEADS * D_IDX), lambda b, i: (b, i, 0)),
                  pl.BlockSpec((1, 1, SUBLANES, t), lambda b, i: (b, i, 0, 0)),
                  pl.BlockSpec((1, nh, s, LANES), lambda b, i: (b, 0, 0, 0), pipeline_mode=pl.Buffered(1)),
                  pl.BlockSpec((1, nh, nt, V_ROWS, t), lambda b, i: (b, 0, 0, 0, 0),
                               pipeline_mode=pl.Buffered(1)),
                  pl.BlockSpec((1, s, LANES), lambda b, i: (b, 0, 0), pipeline_mode=pl.Buffered(1))],
        out_specs=pl.BlockSpec((1, t, nh * D_HEAD), lambda b, i: (b, i, 0)),
        out_shape=jax.ShapeDtypeStruct((bsz, s, nh * D_HEAD), BF16),
        scratch_shapes=[pltpu.VMEM((nt, t, t), F32), pltpu.VMEM((SUBLANES, t), F32),
                        pltpu.VMEM((nh // hg, t, hg * D_HEAD), BF16)]
        + [pltpu.VMEM((t, t), F32)] * hg,
        compiler_params=_cparams(("arbitrary", "arbitrary")),
        name="dsa_attn",
    )(qx, iq, auxt, kx, vt, ik)


def _ffn_kernel(x_ref, yd_ref, yf_ref, dg_ref, xg_ref, g1_ref, sh_ref, sc_ref, g2_ref, ng_ref, fg_ref,
                wo_ref, wg_ref, wu_ref, wd_ref, o_ref, *, hd):
    dot = functools.partial(jnp.dot, preferred_element_type=F32)
    yd = _rms(yd_ref[0].astype(F32), dg_ref[...]).astype(BF16)
    yf = _rms(yf_ref[0].astype(F32), xg_ref[...]).astype(BF16)
    x1 = x_ref[0] + g1_ref[0] * (dot(yd, wo_ref[0:hd, :]) + dot(yf, wo_ref[hd:, :]))
    h = (_rms(x1, ng_ref[...]) * (1.0 + sc_ref[0]) + sh_ref[0]).astype(BF16)
    gate = dot(h, wg_ref[...])
    act = gate * jax.nn.sigmoid(gate) * dot(h, wu_ref[...])
    x2 = x1 + g2_ref[0] * dot(act.astype(BF16), wd_ref[...])
    o_ref[0] = _rms(x2, fg_ref[...])


def _ffn(x, yd, yf, dg, xg, g1, sh, sc, g2, ng, fg, wo, wg, wu, wd, *, tm):
    bsz, s, d = x.shape
    hd = yd.shape[-1]
    tok = lambda width: pl.BlockSpec((1, tm, width), lambda b, i: (b, i, 0))
    per_b = pl.BlockSpec((1, 1, d), lambda b, i: (b, 0, 0))
    const = lambda shape: pl.BlockSpec(shape, lambda b, i: (0, 0), pipeline_mode=pl.Buffered(1))
    return pl.pallas_call(
        functools.partial(_ffn_kernel, hd=hd),
        grid=(bsz, s // tm),
        in_specs=[tok(d), tok(hd), tok(hd), const((1, hd)), const((1, hd)), per_b, per_b, per_b, per_b,
                  const((1, d)), const((1, d)),
                  const(wo.shape), const(wg.shape), const(wu.shape), const(wd.shape)],
        out_specs=tok(d),
        out_shape=jax.ShapeDtypeStruct((bsz, s, d), F32),
        compiler_params=_cparams(("arbitrary", "arbitrary")),
        name="outproj_ffn",
    )(x, yd, yf, dg, xg, g1, sh, sc, g2, ng, fg, wo, wg, wu, wd)


def _pick_tile(n, target):
    t = min(n, target)
    while n % t:
        t //= 2
    return t


def kernel(x, c, positions, ada_w, ada_b, norm_attn_g, w_in, b_forget, dsa_norm_g, fox_norm_g,
           w_out, norm_ffn_g, w_gate, w_up, w_down, final_norm_g):
    bsz, s, d = x.shape
    depth = ada_w.shape[0]
    n_fox = b_forget.shape[1]
    hd = dsa_norm_g.shape[1]
    assert depth == 1, "the fused final RMSNorm assumes a single layer"
    assert n_fox == hd // D_HEAD == 8 and hd % LANES == 0
    k_top = min(TOP_K_MAX, s // 4)
    t_attn = _pick_tile(s, 512)
    tm_in = _pick_tile(s, 512)
    tm_ffn = _pick_tile(s, 512)

    sizes = (hd, hd, hd, N_IDX_HEADS * D_IDX, D_IDX, N_IDX_HEADS, hd, hd, hd, n_fox)
    pts = np.cumsum((0,) + sizes)
    wl = w_in[0]
    seg = lambda k: wl[:, pts[k]:pts[k + 1]]
    small = jnp.zeros((d, LANES), F32)
    small = small.at[:, 0:N_IDX_HEADS].set(seg(5)).at[:, 8:8 + n_fox].set(seg(9))
    w_cat = jnp.concatenate([seg(0), seg(1), seg(2), seg(6), seg(7), seg(8),
                             seg(3), seg(4), seg(4), small], axis=1).astype(BF16)
    bfv = jnp.zeros((1, LANES), F32).at[0, 8:8 + n_fox].set(b_forget[0])

    lane = jnp.arange(LANES)
    inv_freq = ROPE_THETA ** (-jnp.arange(0, ROT_DIM, 2, dtype=F32) / ROT_DIM)
    invf = jnp.where((lane % D_HEAD) < ROT_DIM, inv_freq[lane % (ROT_DIM // 2)], 0.0).reshape(1, LANES)
    pos = positions.reshape(bsz, s, 1)

    mod = _modulation(c, ada_w[0], ada_b[0])
    sh1, sc1, g1, sh2, sc2, g2 = [m.reshape(bsz, 1, d) for m in jnp.split(mod, N_MOD, axis=-1)]

    qxd, kxd, vtd, qxf, kxf, vtf, iq, ik, auxt = _inproj(
        x, sh1, sc1, norm_attn_g[0].reshape(1, d), pos, invf, w_cat, bfv, hd=hd, tm=tm_in, tk=t_attn)

    yf = _fox(qxf, kxf, vtf, t=t_attn, hg=8)
    yd = _dsa(qxd, iq, auxt, kxd, vtd, ik, t=t_attn, k_top=k_top, hg=8)

    return _ffn(x, yd, yf, dsa_norm_g[0].reshape(1, hd), fox_norm_g[0].reshape(1, hd),
                g1, sh2, sc2, g2, norm_ffn_g[0].reshape(1, d), final_norm_g.reshape(1, d),
                w_out[0].astype(BF16), w_gate[0].astype(BF16), w_up[0].astype(BF16),
                w_down[0].astype(BF16), tm=tm_ffn)
```

```python
import functools

import numpy as np
import jax
import jax.numpy as jnp
from jax import lax
from jax.experimental import pallas as pl
from jax.experimental.pallas import tpu as pltpu

F32 = jnp.float32
BF16 = jnp.bfloat16

D_HEAD = 64
N_IDX_HEADS = 4
D_IDX = 64
TOP_K_MAX = 256
ROPE_THETA = 500000.0
ROT_DIM = D_HEAD // 4
N_MOD = 6
RMS_EPS = 1e-6

LOG2E = 1.4426950408889634
V_ROWS = 80
LANES = 128
SUBLANES = 8
VMEM_LIMIT = 56 * 1024 * 1024

F32_MAX = float(np.finfo(np.float32).max)
F32_TINY = float(np.finfo(np.float32).tiny)
MAX_NARROW_STEPS = 48
INT_MIN = -(2 ** 31)
KEY_NEG_INF = int(np.int32(np.uint32(0xFF800000) ^ np.uint32(0x7FFFFFFF)))

NT_DIMS = (((1,), (1,)), ((), ()))


def _cparams(sem):
    return pltpu.CompilerParams(dimension_semantics=sem, vmem_limit_bytes=VMEM_LIMIT)


def _split3(a):
    a0 = a.astype(BF16).astype(F32)
    a1 = (a - a0).astype(BF16).astype(F32)
    return a0, a1, a - a0 - a1


def _rms(x, g):
    return x * lax.rsqrt(jnp.mean(x * x, axis=-1, keepdims=True) + RMS_EPS) * g


def _mod_kernel(c_ref, w_ref, b_ref, o_ref):
    c = c_ref[...]
    a0, a1, _ = _split3(c * jax.nn.sigmoid(c))
    w0, w1, _ = _split3(w_ref[...])
    dot = lambda a, b: jnp.dot(a.astype(BF16), b.astype(BF16), preferred_element_type=F32)
    o_ref[...] = dot(a0, w0) + (dot(a0, w1) + dot(a1, w0)) + b_ref[...]


def _modulation(c, w, b):
    bsz, d = c.shape
    n = w.shape[1]
    tn = 1024
    return pl.pallas_call(
        _mod_kernel,
        grid=(n // tn,),
        in_specs=[pl.BlockSpec((bsz, d), lambda j: (0, 0)),
                  pl.BlockSpec((d, tn), lambda j: (0, j)),
                  pl.BlockSpec((1, tn), lambda j: (0, j))],
        out_specs=pl.BlockSpec((bsz, tn), lambda j: (0, j)),
        out_shape=jax.ShapeDtypeStruct((bsz, n), F32),
        compiler_params=_cparams(("arbitrary",)),
        name="adaln_mod",
    )(c, w, b.reshape(1, n))


def _inproj_kernel(x_ref, sh_ref, sc_ref, g_ref, pos_ref, invf_ref, w_ref, bf_ref,
                   qxd_ref, kxd_ref, vtd_ref, qxf_ref, kxf_ref, vtf_ref, iq_ref, ik_ref, auxt_ref,
                   carry_ref, *, hd, tm, tk, th):
    i = pl.program_id(1)

    @pl.when(i == 0)
    def _():
        carry_ref[...] = jnp.zeros_like(carry_ref)

    lane = lax.broadcasted_iota(jnp.int32, (th, LANES), 1)
    first_half = (lane & (D_HEAD - 1)) < (ROT_DIM // 2)
    in_head = lane < D_HEAD
    one_row = jnp.where(lane == D_HEAD, 1.0, 0.0)
    ones3 = jnp.where(lane < D_HEAD + 3, 1.0, 0.0)
    r = lax.broadcasted_iota(jnp.int32, (th, th), 0)
    c = lax.broadcasted_iota(jnp.int32, (th, th), 1)
    tri = jnp.where(c <= r, 1.0, 0.0).astype(BF16)
    qscale = (D_HEAD ** -0.5) * LOG2E

    def head_of(x, h):
        return pltpu.roll(x, D_HEAD, 1) if h % 2 else x

    for r0 in range(0, tm, th):
        rows = slice(r0, r0 + th)
        u, c0 = r0 // tk, r0 % tk
        h = _rms(x_ref[0, rows], g_ref[...]) * (1.0 + sc_ref[0]) + sh_ref[0]
        proj = jnp.dot(h.astype(BF16), w_ref[...], preferred_element_type=F32)

        ang = pos_ref[0, rows].astype(F32) * invf_ref[...]
        cosv = jnp.cos(ang)
        sinv = jnp.sin(ang)
        sin_signed = jnp.where(first_half, -sinv, sinv)

        def rope(x):
            partner = jnp.where(first_half, pltpu.roll(x, LANES - ROT_DIM // 2, 1),
                                pltpu.roll(x, ROT_DIM // 2, 1))
            return x * cosv + partner * sin_signed

        def slab(base, j):
            return proj[:, base + j * LANES: base + (j + 1) * LANES]

        def put_vt(ref, h, v_slab):
            vt = jnp.where(in_head, head_of(v_slab, h), one_row).T[0:V_ROWS, :]
            ref[0, h, u, :, c0:c0 + th] = vt.astype(BF16)

        for j in range(hd // LANES):
            rq = rope(slab(0, j)) * qscale
            rk = rope(slab(hd, j))
            for h in (2 * j, 2 * j + 1):
                qxd_ref[0, h, rows] = jnp.where(in_head, head_of(rq, h), 0.0).astype(BF16)
                kxd_ref[0, h, rows] = jnp.where(in_head, head_of(rk, h), 0.0).astype(BF16)
                put_vt(vtd_ref, h, slab(2 * hd, j))
        base = 6 * hd
        for j in range(N_IDX_HEADS * D_IDX // LANES):
            iq_ref[0, rows, j * LANES:(j + 1) * LANES] = (rope(slab(base, j)) * (D_IDX ** -0.5)).astype(BF16)
        base += N_IDX_HEADS * D_IDX
        ik_ref[0, rows] = rope(slab(base, 0)).astype(BF16)
        base += LANES

        small = slab(base, 0)
        z = small + bf_ref[...]
        logf = jnp.minimum(z, 0.0) - jnp.log1p(jnp.exp(-jnp.abs(z)))
        logf = jnp.where((lane >= 8) & (lane < 16), logf, 0.0)
        dot = lambda a, b: jnp.dot(a, b.astype(BF16), preferred_element_type=F32)
        l0, l1, l2 = _split3(logf)
        cum = dot(tri, l0) + (dot(tri, l1) + dot(tri, l2)) + carry_ref[...]
        carry_ref[...] = cum[th - 1:th, :]

        aux = jnp.where(lane < N_IDX_HEADS, small * (N_IDX_HEADS ** -0.5), 0.0)
        auxt_ref[0, u, :, c0:c0 + th] = aux.T[0:SUBLANES, :]

        ncum = cum * (-LOG2E)
        for j in range(hd // LANES):
            for h in (2 * j, 2 * j + 1):
                qxf_ref[0, h, rows] = jnp.where(in_head, head_of(slab(3 * hd, j), h) * qscale, ones3).astype(BF16)
                p0, p1, p2 = _split3(ncum[:, 8 + h:9 + h])
                extra = jnp.where(lane == D_HEAD, p0,
                                  jnp.where(lane == D_HEAD + 1, p1, jnp.where(lane == D_HEAD + 2, p2, 0.0)))
                kxf_ref[0, h, rows] = jnp.where(in_head, head_of(slab(4 * hd, j), h), extra).astype(BF16)
                put_vt(vtf_ref, h, slab(5 * hd, j))


def _inproj(x, sh, sc, g, pos, invf, w, bfv, *, hd, tm, tk):
    bsz, s, d = x.shape
    n_cols = w.shape[1]
    nh = hd // D_HEAD
    tok = lambda width: pl.BlockSpec((1, tm, width), lambda b, i: (b, i, 0))
    per_b = pl.BlockSpec((1, 1, d), lambda b, i: (b, 0, 0))
    const = lambda shape: pl.BlockSpec(shape, lambda b, i: (0,) * len(shape), pipeline_mode=pl.Buffered(1))
    headed = jax.ShapeDtypeStruct((bsz, nh, s, LANES), BF16)
    headed_spec = pl.BlockSpec((1, nh, tm, LANES), lambda b, i: (b, 0, i, 0))
    vt = jax.ShapeDtypeStruct((bsz, nh, s // tk, V_ROWS, tk), BF16)
    vt_spec = pl.BlockSpec((1, nh, tm // tk, V_ROWS, tk), lambda b, i: (b, 0, i, 0, 0))
    out_shape = [headed, headed, vt, headed, headed, vt,
                 jax.ShapeDtypeStruct((bsz, s, N_IDX_HEADS * D_IDX), BF16),
                 jax.ShapeDtypeStruct((bsz, s, LANES), BF16),
                 jax.ShapeDtypeStruct((bsz, s // tk, SUBLANES, tk), F32)]
    out_specs = [headed_spec, headed_spec, vt_spec, headed_spec, headed_spec, vt_spec,
                 tok(N_IDX_HEADS * D_IDX), tok(LANES),
                 pl.BlockSpec((1, tm // tk, SUBLANES, tk), lambda b, i: (b, i, 0, 0))]
    return pl.pallas_call(
        functools.partial(_inproj_kernel, hd=hd, tm=tm, tk=tk, th=min(tm, 256)),
        grid=(bsz, s // tm),
        in_specs=[tok(d), per_b, per_b, const((1, d)), tok(1), const((1, LANES)),
                  const((d, n_cols)), const((1, LANES))],
        out_specs=out_specs,
        out_shape=out_shape,
        scratch_shapes=[pltpu.VMEM((1, LANES), F32)],
        compiler_params=_cparams(("arbitrary", "arbitrary")),
        name="inproj",
    )(x, sh, sc, g, pos, invf, w, bfv)


def _flash_t(q, kx_ref, vt_ref, heads, n_plain, t, plain_fn, last_fn, s_refs):
    nh = len(heads)

    def produce(e, kb, fn):
        off = pl.multiple_of(kb * t, t)
        st = fn(lax.dot_general(kx_ref[0, heads[e], pl.ds(off, t), :], q[e], NT_DIMS,
                                preferred_element_type=F32), kb)
        s_refs[e][...] = st
        return jnp.max(st, axis=0, keepdims=True)

    def consume(e, kb, state, bmax):
        m, acc = state
        m_new = jnp.maximum(m, bmax)
        pt = jnp.exp2(s_refs[e][...] - m_new).astype(BF16)
        acc = jnp.exp2(m - m_new) * acc + jnp.dot(vt_ref[0, heads[e], kb], pt, preferred_element_type=F32)
        return m_new, acc

    def body(kb, carry):
        states, bmaxes = carry
        cur = jnp.where(kb == 0, n_plain, kb - 1)
        new_states, new_bmaxes = [], []
        for e in range(nh):
            new_states.append(consume(e, cur, states[e], bmaxes[e]))
            new_bmaxes.append(produce(e, kb, plain_fn))
        return tuple(new_states), tuple(new_bmaxes)

    states = tuple((jnp.full((1, t), -1e30, F32), jnp.zeros((V_ROWS, t), F32)) for _ in heads)
    bmaxes = tuple(produce(e, n_plain, last_fn) for e in range(nh))
    states, bmaxes = lax.fori_loop(0, n_plain, body, (states, bmaxes))
    cur = jnp.maximum(n_plain - 1, 0)
    states = [consume(e, cur, states[e], bmaxes[e]) for e in range(nh)]
    o = jnp.concatenate([acc[0:D_HEAD] / acc[D_HEAD:D_HEAD + 1] for _, acc in states], axis=0)
    return o.T


def _fox_kernel(qx_ref, kx_ref, vt_ref, o_ref, *s_refs, t, hg):
    qi = pl.program_id(2)
    causal = lax.broadcasted_iota(jnp.int32, (t, t), 0) <= lax.broadcasted_iota(jnp.int32, (t, t), 1)
    o = _flash_t([qx_ref[0, e] for e in range(hg)], kx_ref, vt_ref, tuple(range(hg)), qi, t,
                 lambda st, kb: st, lambda st, kb: jnp.where(causal, st, -jnp.inf), s_refs)
    o_ref[0] = o.astype(o_ref.dtype)


def _fox(qx, kx, vt, *, t, hg):
    bsz, nh, s, _ = qx.shape
    nt = s // t
    return pl.pallas_call(
        functools.partial(_fox_kernel, t=t, hg=hg),
        grid=(bsz, nh // hg, nt),
        in_specs=[pl.BlockSpec((1, hg, t, LANES), lambda b, j, i: (b, j, i, 0)),
                  pl.BlockSpec((1, hg, s, LANES), lambda b, j, i: (b, j, 0, 0)),
                  pl.BlockSpec((1, hg, nt, V_ROWS, t), lambda b, j, i: (b, j, 0, 0, 0))],
        out_specs=pl.BlockSpec((1, t, hg * D_HEAD), lambda b, j, i: (b, i, j)),
        out_shape=jax.ShapeDtypeStruct((bsz, s, nh * D_HEAD), BF16),
        scratch_shapes=[pltpu.VMEM((t, t), F32)] * hg,
        compiler_params=_cparams(("arbitrary", "arbitrary", "arbitrary")),
        name="fox_attn",
    )(qx, kx, vt)


def _dsa_kernel(qx_ref, iq_ref, auxt_ref, kx_ref, vt_ref, ik_ref, o_ref, sc_ref, thr_ref, og_ref, *s_refs,
                t, n_heads, k_top):
    qi = pl.program_id(1)
    n_kb = qi + 1
    kidx = lax.broadcasted_iota(jnp.int32, (t, t), 0)
    qidx = lax.broadcasted_iota(jnp.int32, (t, t), 1)
    lane = lax.broadcasted_iota(jnp.int32, (t, LANES), 1)
    iw = auxt_ref[0, 0]

    iqm = []
    for h in range(N_IDX_HEADS):
        iq2 = iq_ref[0, :, (h // 2) * LANES:(h // 2 + 1) * LANES]
        keep = (lane < D_IDX) if h % 2 == 0 else (lane >= D_IDX)
        iqm.append(iq2 * jnp.where(keep, 1.0, 0.0).astype(BF16))

    def score_block(kb, masked, stats):
        off = pl.multiple_of(kb * t, t)
        ik2 = ik_ref[0, pl.ds(off, t), :]
        sc = jnp.zeros((t, t), F32)
        for h in range(N_IDX_HEADS):
            d = lax.dot_general(ik2, iqm[h], NT_DIMS, preferred_element_type=F32)
            sc = sc + jnp.maximum(d, 0.0) * iw[h:h + 1, :]
        sc_min = sc
        if masked:
            causal = kidx <= qidx
            sc_min = jnp.where(causal, sc, jnp.inf)
            sc = jnp.where(causal, sc, -jnp.inf)
        sc_ref[kb] = sc
        col = lambda x, op: op(x, axis=0, keepdims=True)
        return (jnp.minimum(stats[0], col(sc_min, jnp.min)), jnp.maximum(stats[1], col(sc, jnp.max)),
                stats[2] + col(jnp.where(sc >= 0.0, 1.0, 0.0), jnp.sum),
                stats[3] + col(jnp.where(sc > 0.0, 1.0, 0.0), jnp.sum))

    zeros = jnp.zeros((1, t), F32)
    stats = (jnp.full((1, t), jnp.inf, F32), jnp.full((1, t), -jnp.inf, F32), zeros, zeros)
    stats = lax.fori_loop(0, qi, lambda kb, c: score_block(kb, False, c), stats)
    smin, smax, c0_ge, c0_gt = score_block(qi, True, stats)

    def count(cmp_fn):
        def body(kb, tot):
            return tot + jnp.sum(jnp.where(cmp_fn(sc_ref[kb]), 1.0, 0.0), axis=0, keepdims=True)
        return lax.fori_loop(0, n_kb, body, jnp.zeros((1, t), F32))

    kf = float(k_top)
    n_causal = (qi * t + 1 + lax.broadcasted_iota(jnp.int32, (1, t), 1)).astype(F32)
    few = n_causal <= kf
    positive = c0_gt >= kf
    at_zero = jnp.logical_and(jnp.logical_not(positive), c0_ge >= kf)
    hi_pos = smax + jnp.maximum(jnp.abs(smax) * 1e-6, 1e-30)
    lo = jnp.where(few, -F32_MAX, jnp.where(at_zero, 0.0, jnp.where(positive, F32_TINY, smin)))
    c_lo = jnp.where(few, n_causal, jnp.where(at_zero, c0_ge, jnp.where(positive, c0_gt, n_causal)))
    hi = jnp.where(positive, hi_pos, -F32_TINY)
    c_hi = jnp.where(positive, 0.0, c0_ge)
    done = jnp.where(jnp.logical_or(jnp.logical_or(few, at_zero), c_lo <= kf), 1.0, 0.0)

    log_k = float(np.log(kf))
    f_of = lambda cnt: jnp.log(jnp.maximum(cnt, 0.5)) - log_k

    def narrow(carry):
        it, _, lo, hi, c_lo, c_hi, f_lo, f_hi, side, done = carry
        mid = 0.5 * lo + 0.5 * hi
        stuck = jnp.logical_or(mid <= lo, mid >= hi)
        interp = lo + (hi - lo) * jnp.clip(f_lo / jnp.maximum(f_lo - f_hi, 1e-6), 0.01, 0.99)
        cand = jnp.where(jnp.logical_or(interp <= lo, interp >= hi), mid, interp)
        c = count(lambda x: x >= cand)
        f_c = f_of(c)
        active = jnp.logical_and(done < 0.5, jnp.logical_not(stuck))
        up = jnp.logical_and(active, c >= kf)
        down = jnp.logical_and(active, c < kf)
        f_hi = jnp.where(jnp.logical_and(up, side > 0.5), 0.5 * f_hi, f_hi)
        f_lo = jnp.where(jnp.logical_and(down, side < -0.5), 0.5 * f_lo, f_lo)
        lo, c_lo, f_lo = jnp.where(up, cand, lo), jnp.where(up, c, c_lo), jnp.where(up, f_c, f_lo)
        hi, c_hi, f_hi = jnp.where(down, cand, hi), jnp.where(down, c, c_hi), jnp.where(down, f_c, f_hi)
        side = jnp.where(up, 1.0, jnp.where(down, -1.0, side))
        done = jnp.where(jnp.logical_or(stuck, jnp.logical_or(c_lo <= kf, c_lo - c_hi <= 2.0)), 1.0, done)
        return it + 1, jnp.min(done), lo, hi, c_lo, c_hi, f_lo, f_hi, side, done

    carry = (jnp.int32(0), jnp.min(done), lo, hi, c_lo, c_hi, f_of(c_lo), f_of(c_hi),
             jnp.zeros((1, t), F32), done)
    carry = lax.while_loop(lambda c: jnp.logical_and(c[0] < MAX_NARROW_STEPS, c[1] < 0.5), narrow, carry)
    _, all_done, lo, hi, c_lo, _, _, _, _, _ = carry
    thr_ref[0:1, :] = lo
    thr_ref[1:2, :] = c_lo

    extract = jnp.logical_and(c_lo > kf, jnp.logical_not(jnp.logical_or(few, at_zero)))

    @pl.when(jnp.logical_and(all_done > 0.5, jnp.max(jnp.where(extract, 1.0, 0.0)) > 0.5))
    def _():
        def body(kb, best):
            x = sc_ref[kb]
            return jnp.maximum(best, jnp.max(jnp.where(x < hi, x, -jnp.inf), axis=0, keepdims=True))
        below = lax.fori_loop(0, n_kb, body, jnp.full((1, t), -jnp.inf, F32))
        kth = jnp.where(extract, below, lo)
        thr_ref[0:1, :] = kth
        thr_ref[1:2, :] = count(lambda x: x >= kth)

    @pl.when(all_done < 0.5)
    def _():
        def to_float(key):
            return pltpu.bitcast(key ^ ((key >> 31) & 0x7FFFFFFF), F32)

        def search(it, u):
            bit = lax.shift_left(jnp.int32(1), 31 - it)
            cand = to_float((u | bit) ^ INT_MIN)
            return jnp.where(count(lambda x: x >= cand) >= kf, u | bit, u)

        u = lax.fori_loop(0, 32, search, jnp.zeros((1, t), jnp.int32))
        thr_bits = to_float(jnp.maximum(u ^ INT_MIN, KEY_NEG_INF + 1))
        thr_ref[0:1, :] = thr_bits
        thr_ref[1:2, :] = count(lambda x: x >= thr_bits)

    thr = thr_ref[0:1, :]
    has_ties = jnp.max(thr_ref[1:2, :]) > kf

    @pl.when(jnp.logical_not(has_ties))
    def _():
        def body(kb, _):
            sc_ref[kb] = jnp.where(sc_ref[kb] >= thr, 0.0, -jnp.inf)
            return 0
        lax.fori_loop(0, n_kb, body, 0)

    @pl.when(has_ties)
    def _():
        need = kf - count(lambda x: x > thr)
        earlier = jnp.where(qidx < kidx, 1.0, 0.0).astype(BF16)

        def body(kb, run):
            x = sc_ref[kb]
            eq = jnp.where(x == thr, 1.0, 0.0).astype(BF16)
            room = need - run
            rank = jnp.dot(earlier, eq, preferred_element_type=F32)
            tie_mask = jnp.where(rank < room, 0.0, -jnp.inf)
            sc_ref[kb] = jnp.where(x > thr, 0.0, jnp.where(x == thr, tie_mask, -jnp.inf))
            return run + rank[t - 1:t, :] + eq[t - 1:t, :].astype(F32)
        lax.fori_loop(0, n_kb, body, jnp.zeros((1, t), F32))

    add_mask = lambda st, kb: st + sc_ref[kb]
    hg = len(s_refs)

    def head_group(j, _):
        heads = [j * hg + e for e in range(hg)]
        o = _flash_t([qx_ref[0, h] for h in heads], kx_ref, vt_ref, heads, qi, t, add_mask, add_mask, s_refs)
        og_ref[j] = o.astype(og_ref.dtype)
        return 0

    lax.fori_loop(0, n_heads // hg, head_group, 0)
    for j in range(n_heads // hg):
        o_ref[0, :, j * hg * D_HEAD:(j + 1) * hg * D_HEAD] = og_ref[j]


def _dsa(qx, iq, auxt, kx, vt, ik, *, t, k_top, hg):
    bsz, nh, s, _ = qx.shape
    nt = s // t
    return pl.pallas_call(
        functools.partial(_dsa_kernel, t=t, n_heads=nh, k_top=k_top),
        grid=(bsz, nt),
        in_specs=[pl.BlockSpec((1, nh, t, LANES), lambda b, i: (b, 0, i, 0)),
                  pl.BlockSpec((1, t, N_IDX_HEADS * D_IDX), lambda b, i: (b, i, 0)),
                  pl.BlockSpec((1, 1, SUBLANES, t), lambda b, i: (b, i, 0, 0)),
                  pl.BlockSpec((1, nh, s, LANES), lambda b, i: (b, 0, 0, 0), pipeline_mode=pl.Buffered(1)),
                  pl.BlockSpec((1, nh, nt, V_ROWS, t), lambda b, i: (b, 0, 0, 0, 0),
                               pipeline_mode=pl.Buffered(1)),
                  pl.BlockSpec((1, s, LANES), lambda b, i: (b, 0, 0), pipeline_mode=pl.Buffered(1))],
        out_specs=pl.BlockSpec((1, t, nh * D_HEAD), lambda b, i: (b, i, 0)),
        out_shape=jax.ShapeDtypeStruct((bsz, s, nh * D_HEAD), BF16),
        scratch_shapes=[pltpu.VMEM((nt, t, t), F32), pltpu.VMEM((SUBLANES, t), F32),
                        pltpu.VMEM((nh // hg, t, hg * D_HEAD), BF16)]
        + [pltpu.VMEM((t, t), F32)] * hg,
        compiler_params=_cparams(("arbitrary", "arbitrary")),
        name="dsa_attn",
    )(qx, iq, auxt, kx, vt, ik)


def _ffn_kernel(x_ref, yd_ref, yf_ref, dg_ref, xg_ref, g1_ref, sh_ref, sc_ref, g2_ref, ng_ref, fg_ref,
                wo_ref, wg_ref, wu_ref, wd_ref, o_ref, *, hd):
    dot = functools.partial(jnp.dot, preferred_element_type=F32)
    yd = _rms(yd_ref[0].astype(F32), dg_ref[...]).astype(BF16)
    yf = _rms(yf_ref[0].astype(F32), xg_ref[...]).astype(BF16)
    x1 = x_ref[0] + g1_ref[0] * (dot(yd, wo_ref[0:hd, :]) + dot(yf, wo_ref[hd:, :]))
    h = (_rms(x1, ng_ref[...]) * (1.0 + sc_ref[0]) + sh_ref[0]).astype(BF16)
    gate = dot(h, wg_ref[...])
    act = gate * jax.nn.sigmoid(gate) * dot(h, wu_ref[...])
    x2 = x1 + g2_ref[0] * dot(act.astype(BF16), wd_ref[...])
    o_ref[0] = _rms(x2, fg_ref[...])


def _ffn(x, yd, yf, dg, xg, g1, sh, sc, g2, ng, fg, wo, wg, wu, wd, *, tm):
    bsz, s, d = x.shape
    hd = yd.shape[-1]
    tok = lambda width: pl.BlockSpec((1, tm, width), lambda b, i: (b, i, 0))
    per_b = pl.BlockSpec((1, 1, d), lambda b, i: (b, 0, 0))
    const = lambda shape: pl.BlockSpec(shape, lambda b, i: (0, 0), pipeline_mode=pl.Buffered(1))
    return pl.pallas_call(
        functools.partial(_ffn_kernel, hd=hd),
        grid=(bsz, s // tm),
        in_specs=[tok(d), tok(hd), tok(hd), const((1, hd)), const((1, hd)), per_b, per_b, per_b, per_b,
                  const((1, d)), const((1, d)),
                  const(wo.shape), const(wg.shape), const(wu.shape), const(wd.shape)],
        out_specs=tok(d),
        out_shape=jax.ShapeDtypeStruct((bsz, s, d), F32),
        compiler_params=_cparams(("arbitrary", "arbitrary")),
        name="outproj_ffn",
    )(x, yd, yf, dg, xg, g1, sh, sc, g2, ng, fg, wo, wg, wu, wd)


def _pick_tile(n, target):
    t = min(n, target)
    while n % t:
        t //= 2
    return t


def kernel(x, c, positions, ada_w, ada_b, norm_attn_g, w_in, b_forget, dsa_norm_g, fox_norm_g,
           w_out, norm_ffn_g, w_gate, w_up, w_down, final_norm_g):
    bsz, s, d = x.shape
    depth = ada_w.shape[0]
    n_fox = b_forget.shape[1]
    hd = dsa_norm_g.shape[1]
    assert depth == 1, "the fused final RMSNorm assumes a single layer"
    assert n_fox == hd // D_HEAD == 8 and hd % LANES == 0
    k_top = min(TOP_K_MAX, s // 4)
    t_attn = _pick_tile(s, 512)
    tm_in = _pick_tile(s, 1024)
    tm_ffn = _pick_tile(s, 512)

    sizes = (hd, hd, hd, N_IDX_HEADS * D_IDX, D_IDX, N_IDX_HEADS, hd, hd, hd, n_fox)
    pts = np.cumsum((0,) + sizes)
    wl = w_in[0]
    seg = lambda k: wl[:, pts[k]:pts[k + 1]]
    small = jnp.zeros((d, LANES), F32)
    small = small.at[:, 0:N_IDX_HEADS].set(seg(5)).at[:, 8:8 + n_fox].set(seg(9))
    w_cat = jnp.concatenate([seg(0), seg(1), seg(2), seg(6), seg(7), seg(8),
                             seg(3), seg(4), seg(4), small], axis=1).astype(BF16)
    bfv = jnp.zeros((1, LANES), F32).at[0, 8:8 + n_fox].set(b_forget[0])

    lane = jnp.arange(LANES)
    inv_freq = ROPE_THETA ** (-jnp.arange(0, ROT_DIM, 2, dtype=F32) / ROT_DIM)
    invf = jnp.where((lane % D_HEAD) < ROT_DIM, inv_freq[lane % (ROT_DIM // 2)], 0.0).reshape(1, LANES)
    pos = positions.reshape(bsz, s, 1)

    mod = _modulation(c, ada_w[0], ada_b[0])
    sh1, sc1, g1, sh2, sc2, g2 = [m.reshape(bsz, 1, d) for m in jnp.split(mod, N_MOD, axis=-1)]

    qxd, kxd, vtd, qxf, kxf, vtf, iq, ik, auxt = _inproj(
        x, sh1, sc1, norm_attn_g[0].reshape(1, d), pos, invf, w_cat, bfv, hd=hd, tm=tm_in, tk=t_attn)

    yf = _fox(qxf, kxf, vtf, t=t_attn, hg=8)
    yd = _dsa(qxd, iq, auxt, kxd, vtd, ik, t=t_attn, k_top=k_top, hg=8)

    return _ffn(x, yd, yf, dsa_norm_g[0].reshape(1, hd), fox_norm_g[0].reshape(1, hd),
                g1, sh2, sc2, g2, norm_ffn_g[0].reshape(1, d), final_norm_g.reshape(1, d),
                w_out[0].astype(BF16), w_gate[0].astype(BF16), w_up[0].astype(BF16),
                w_down[0].astype(BF16), tm=tm_ffn)
```

```python
import functools

import numpy as np
import jax
import jax.numpy as jnp
from jax import lax
from jax.experimental import pallas as pl
from jax.experimental.pallas import tpu as pltpu

F32 = jnp.float32
BF16 = jnp.bfloat16

D_HEAD = 64
N_IDX_HEADS = 4
D_IDX = 64
TOP_K_MAX = 256
ROPE_THETA = 500000.0
ROT_DIM = D_HEAD // 4
N_MOD = 6
RMS_EPS = 1e-6

LOG2E = 1.4426950408889634
V_ROWS = 80
LANES = 128
SUBLANES = 8
VMEM_LIMIT = 56 * 1024 * 1024

F32_MAX = float(np.finfo(np.float32).max)
F32_TINY = float(np.finfo(np.float32).tiny)
FLASH_LATE_HEADS = 1
MAX_NARROW_STEPS = 48
INT_MIN = -(2 ** 31)
KEY_NEG_INF = int(np.int32(np.uint32(0xFF800000) ^ np.uint32(0x7FFFFFFF)))

NT_DIMS = (((1,), (1,)), ((), ()))


def _cparams(sem):
    return pltpu.CompilerParams(dimension_semantics=sem, vmem_limit_bytes=VMEM_LIMIT)


def _split3(a):
    a0 = a.astype(BF16).astype(F32)
    a1 = (a - a0).astype(BF16).astype(F32)
    return a0, a1, a - a0 - a1


def _rms(x, g):
    return x * lax.rsqrt(jnp.mean(x * x, axis=-1, keepdims=True) + RMS_EPS) * g


def _mod_kernel(c_ref, w_ref, b_ref, o_ref):
    c = c_ref[...]
    a0, a1, _ = _split3(c * jax.nn.sigmoid(c))
    w0, w1, _ = _split3(w_ref[...])
    dot = lambda a, b: jnp.dot(a.astype(BF16), b.astype(BF16), preferred_element_type=F32)
    o_ref[...] = dot(a0, w0) + (dot(a0, w1) + dot(a1, w0)) + b_ref[...]


def _modulation(c, w, b):
    bsz, d = c.shape
    n = w.shape[1]
    tn = 1024
    return pl.pallas_call(
        _mod_kernel,
        grid=(n // tn,),
        in_specs=[pl.BlockSpec((bsz, d), lambda j: (0, 0)),
                  pl.BlockSpec((d, tn), lambda j: (0, j)),
                  pl.BlockSpec((1, tn), lambda j: (0, j))],
        out_specs=pl.BlockSpec((bsz, tn), lambda j: (0, j)),
        out_shape=jax.ShapeDtypeStruct((bsz, n), F32),
        compiler_params=_cparams(("arbitrary",)),
        name="adaln_mod",
    )(c, w, b.reshape(1, n))


def _inproj_kernel(x_ref, sh_ref, sc_ref, g_ref, pos_ref, invf_ref, w_ref, bf_ref,
                   qxd_ref, kxd_ref, vtd_ref, qxf_ref, kxf_ref, vtf_ref, iq_ref, ik_ref, auxt_ref,
                   carry_ref, *, hd, tm, tk, th):
    i = pl.program_id(1)

    @pl.when(i == 0)
    def _():
        carry_ref[...] = jnp.zeros_like(carry_ref)

    lane = lax.broadcasted_iota(jnp.int32, (th, LANES), 1)
    first_half = (lane & (D_HEAD - 1)) < (ROT_DIM // 2)
    in_head = lane < D_HEAD
    one_row = jnp.where(lane == D_HEAD, 1.0, 0.0)
    ones3 = jnp.where(lane < D_HEAD + 3, 1.0, 0.0)
    r = lax.broadcasted_iota(jnp.int32, (th, th), 0)
    c = lax.broadcasted_iota(jnp.int32, (th, th), 1)
    tri = jnp.where(c <= r, 1.0, 0.0).astype(BF16)
    qscale = (D_HEAD ** -0.5) * LOG2E

    def head_of(x, h):
        return pltpu.roll(x, D_HEAD, 1) if h % 2 else x

    for r0 in range(0, tm, th):
        rows = slice(r0, r0 + th)
        u, c0 = r0 // tk, r0 % tk
        h = _rms(x_ref[0, rows], g_ref[...]) * (1.0 + sc_ref[0]) + sh_ref[0]
        proj = jnp.dot(h.astype(BF16), w_ref[...], preferred_element_type=F32)

        ang = pos_ref[0, rows].astype(F32) * invf_ref[...]
        cosv = jnp.cos(ang)
        sinv = jnp.sin(ang)
        sin_signed = jnp.where(first_half, -sinv, sinv)

        def rope(x):
            partner = jnp.where(first_half, pltpu.roll(x, LANES - ROT_DIM // 2, 1),
                                pltpu.roll(x, ROT_DIM // 2, 1))
            return x * cosv + partner * sin_signed

        def slab(base, j):
            return proj[:, base + j * LANES: base + (j + 1) * LANES]

        def put_vt(ref, h, v_slab):
            vt = jnp.where(in_head, head_of(v_slab, h), one_row).T[0:V_ROWS, :]
            ref[0, h, u, :, c0:c0 + th] = vt.astype(BF16)

        for j in range(hd // LANES):
            rq = rope(slab(0, j)) * qscale
            rk = rope(slab(hd, j))
            for h in (2 * j, 2 * j + 1):
                qxd_ref[0, h, rows] = jnp.where(in_head, head_of(rq, h), 0.0).astype(BF16)
                kxd_ref[0, h, rows] = jnp.where(in_head, head_of(rk, h), 0.0).astype(BF16)
                put_vt(vtd_ref, h, slab(2 * hd, j))
        base = 6 * hd
        for j in range(N_IDX_HEADS * D_IDX // LANES):
            iq_ref[0, rows, j * LANES:(j + 1) * LANES] = (rope(slab(base, j)) * (D_IDX ** -0.5)).astype(BF16)
        base += N_IDX_HEADS * D_IDX
        ik_ref[0, rows] = rope(slab(base, 0)).astype(BF16)
        base += LANES

        small = slab(base, 0)
        z = small + bf_ref[...]
        logf = jnp.minimum(z, 0.0) - jnp.log1p(jnp.exp(-jnp.abs(z)))
        logf = jnp.where((lane >= 8) & (lane < 16), logf, 0.0)
        dot = lambda a, b: jnp.dot(a, b.astype(BF16), preferred_element_type=F32)
        l0, l1, l2 = _split3(logf)
        cum = dot(tri, l0) + (dot(tri, l1) + dot(tri, l2)) + carry_ref[...]
        carry_ref[...] = cum[th - 1:th, :]

        aux = jnp.where(lane < N_IDX_HEADS, small * (N_IDX_HEADS ** -0.5), 0.0)
        auxt_ref[0, u, :, c0:c0 + th] = aux.T[0:SUBLANES, :]

        ncum = cum * (-LOG2E)
        for j in range(hd // LANES):
            for h in (2 * j, 2 * j + 1):
                qxf_ref[0, h, rows] = jnp.where(in_head, head_of(slab(3 * hd, j), h) * qscale, ones3).astype(BF16)
                p0, p1, p2 = _split3(ncum[:, 8 + h:9 + h])
                extra = jnp.where(lane == D_HEAD, p0,
                                  jnp.where(lane == D_HEAD + 1, p1, jnp.where(lane == D_HEAD + 2, p2, 0.0)))
                kxf_ref[0, h, rows] = jnp.where(in_head, head_of(slab(4 * hd, j), h), extra).astype(BF16)
                put_vt(vtf_ref, h, slab(5 * hd, j))


def _inproj(x, sh, sc, g, pos, invf, w, bfv, *, hd, tm, tk):
    bsz, s, d = x.shape
    n_cols = w.shape[1]
    nh = hd // D_HEAD
    tok = lambda width: pl.BlockSpec((1, tm, width), lambda b, i: (b, i, 0))
    per_b = pl.BlockSpec((1, 1, d), lambda b, i: (b, 0, 0))
    const = lambda shape: pl.BlockSpec(shape, lambda b, i: (0,) * len(shape), pipeline_mode=pl.Buffered(1))
    headed = jax.ShapeDtypeStruct((bsz, nh, s, LANES), BF16)
    headed_spec = pl.BlockSpec((1, nh, tm, LANES), lambda b, i: (b, 0, i, 0))
    vt = jax.ShapeDtypeStruct((bsz, nh, s // tk, V_ROWS, tk), BF16)
    vt_spec = pl.BlockSpec((1, nh, tm // tk, V_ROWS, tk), lambda b, i: (b, 0, i, 0, 0))
    out_shape = [headed, headed, vt, headed, headed, vt,
                 jax.ShapeDtypeStruct((bsz, s, N_IDX_HEADS * D_IDX), BF16),
                 jax.ShapeDtypeStruct((bsz, s, LANES), BF16),
                 jax.ShapeDtypeStruct((bsz, s // tk, SUBLANES, tk), F32)]
    out_specs = [headed_spec, headed_spec, vt_spec, headed_spec, headed_spec, vt_spec,
                 tok(N_IDX_HEADS * D_IDX), tok(LANES),
                 pl.BlockSpec((1, tm // tk, SUBLANES, tk), lambda b, i: (b, i, 0, 0))]
    return pl.pallas_call(
        functools.partial(_inproj_kernel, hd=hd, tm=tm, tk=tk, th=min(tm, 256)),
        grid=(bsz, s // tm),
        in_specs=[tok(d), per_b, per_b, const((1, d)), tok(1), const((1, LANES)),
                  const((d, n_cols)), const((1, LANES))],
        out_specs=out_specs,
        out_shape=out_shape,
        scratch_shapes=[pltpu.VMEM((1, LANES), F32)],
        compiler_params=_cparams(("arbitrary", "arbitrary")),
        name="inproj",
    )(x, sh, sc, g, pos, invf, w, bfv)


def _flash_t(q, kx_ref, vt_ref, heads, n_plain, t, plain_fn, last_fn, s_refs):
    nh = len(heads)

    def produce(e, kb, fn):
        off = pl.multiple_of(kb * t, t)
        st = fn(lax.dot_general(kx_ref[0, heads[e], pl.ds(off, t), :], q[e], NT_DIMS,
                                preferred_element_type=F32), kb)
        s_refs[e][...] = st
        return jnp.max(st, axis=0, keepdims=True)

    def consume(e, kb, state, bmax):
        m, acc = state
        m_new = jnp.maximum(m, bmax)
        pt = jnp.exp2(s_refs[e][...] - m_new).astype(BF16)
        acc = jnp.exp2(m - m_new) * acc + jnp.dot(vt_ref[0, heads[e], kb], pt, preferred_element_type=F32)
        return m_new, acc

    n_late = min(FLASH_LATE_HEADS, nh - 1)
    early, late = range(nh - n_late), range(nh - n_late, nh)

    def body(kb, carry):
        states, bmaxes = carry
        states = list(states)
        cur = jnp.where(kb == 0, n_plain, kb - 1)
        late_bmax = [produce(e, kb, plain_fn) for e in late]
        new_bmaxes = []
        for e in early:
            states[e] = consume(e, cur, states[e], bmaxes[e])
            new_bmaxes.append(produce(e, kb, plain_fn))
        for e, bm in zip(late, late_bmax):
            states[e] = consume(e, kb, states[e], bm)
        return tuple(states), tuple(new_bmaxes)

    states = tuple((jnp.full((1, t), -1e30, F32), jnp.zeros((V_ROWS, t), F32)) for _ in heads)
    bmaxes = tuple(produce(e, n_plain, last_fn) for e in early)
    states, bmaxes = lax.fori_loop(0, n_plain, body, (states, bmaxes))
    states = list(states)
    late_bmax = [produce(e, n_plain, last_fn) for e in late]
    cur = jnp.maximum(n_plain - 1, 0)
    for e in early:
        states[e] = consume(e, cur, states[e], bmaxes[e])
    for e, bm in zip(late, late_bmax):
        states[e] = consume(e, n_plain, states[e], bm)
    o = jnp.concatenate([acc[0:D_HEAD] / acc[D_HEAD:D_HEAD + 1] for _, acc in states], axis=0)
    return o.T


def _fox_kernel(qx_ref, kx_ref, vt_ref, o_ref, *s_refs, t, hg):
    qi = pl.program_id(2)
    causal = lax.broadcasted_iota(jnp.int32, (t, t), 0) <= lax.broadcasted_iota(jnp.int32, (t, t), 1)
    o = _flash_t([qx_ref[0, e] for e in range(hg)], kx_ref, vt_ref, tuple(range(hg)), qi, t,
                 lambda st, kb: st, lambda st, kb: jnp.where(causal, st, -jnp.inf), s_refs)
    o_ref[0] = o.astype(o_ref.dtype)


def _fox(qx, kx, vt, *, t, hg):
    bsz, nh, s, _ = qx.shape
    nt = s // t
    return pl.pallas_call(
        functools.partial(_fox_kernel, t=t, hg=hg),
        grid=(bsz, nh // hg, nt),
        in_specs=[pl.BlockSpec((1, hg, t, LANES), lambda b, j, i: (b, j, i, 0)),
                  pl.BlockSpec((1, hg, s, LANES), lambda b, j, i: (b, j, 0, 0)),
                  pl.BlockSpec((1, hg, nt, V_ROWS, t), lambda b, j, i: (b, j, 0, 0, 0))],
        out_specs=pl.BlockSpec((1, t, hg * D_HEAD), lambda b, j, i: (b, i, j)),
        out_shape=jax.ShapeDtypeStruct((bsz, s, nh * D_HEAD), BF16),
        scratch_shapes=[pltpu.VMEM((t, t), F32)] * hg,
        compiler_params=_cparams(("arbitrary", "arbitrary", "arbitrary")),
        name="fox_attn",
    )(qx, kx, vt)


def _dsa_kernel(qx_ref, iq_ref, auxt_ref, kx_ref, vt_ref, ik_ref, o_ref, sc_ref, thr_ref, og_ref, *s_refs,
                t, n_heads, k_top):
    qi = pl.program_id(1)
    n_kb = qi + 1
    kidx = lax.broadcasted_iota(jnp.int32, (t, t), 0)
    qidx = lax.broadcasted_iota(jnp.int32, (t, t), 1)
    lane = lax.broadcasted_iota(jnp.int32, (t, LANES), 1)
    iw = auxt_ref[0, 0]

    iqm = []
    for h in range(N_IDX_HEADS):
        iq2 = iq_ref[0, :, (h // 2) * LANES:(h // 2 + 1) * LANES]
        keep = (lane < D_IDX) if h % 2 == 0 else (lane >= D_IDX)
        iqm.append(iq2 * jnp.where(keep, 1.0, 0.0).astype(BF16))

    def score_block(kb, masked, stats):
        off = pl.multiple_of(kb * t, t)
        ik2 = ik_ref[0, pl.ds(off, t), :]
        sc = jnp.zeros((t, t), F32)
        for h in range(N_IDX_HEADS):
            d = lax.dot_general(ik2, iqm[h], NT_DIMS, preferred_element_type=F32)
            sc = sc + jnp.maximum(d, 0.0) * iw[h:h + 1, :]
        sc_min = sc
        if masked:
            causal = kidx <= qidx
            sc_min = jnp.where(causal, sc, jnp.inf)
            sc = jnp.where(causal, sc, -jnp.inf)
        sc_ref[kb] = sc
        col = lambda x, op: op(x, axis=0, keepdims=True)
        return (jnp.minimum(stats[0], col(sc_min, jnp.min)), jnp.maximum(stats[1], col(sc, jnp.max)),
                stats[2] + col(jnp.where(sc >= 0.0, 1.0, 0.0), jnp.sum),
                stats[3] + col(jnp.where(sc > 0.0, 1.0, 0.0), jnp.sum))

    zeros = jnp.zeros((1, t), F32)
    stats = (jnp.full((1, t), jnp.inf, F32), jnp.full((1, t), -jnp.inf, F32), zeros, zeros)
    stats = lax.fori_loop(0, qi, lambda kb, c: score_block(kb, False, c), stats)
    smin, smax, c0_ge, c0_gt = score_block(qi, True, stats)

    def count(cmp_fn):
        def body(kb, tot):
            return tot + jnp.sum(jnp.where(cmp_fn(sc_ref[kb]), 1.0, 0.0), axis=0, keepdims=True)
        return lax.fori_loop(0, n_kb, body, jnp.zeros((1, t), F32))

    kf = float(k_top)
    n_causal = (qi * t + 1 + lax.broadcasted_iota(jnp.int32, (1, t), 1)).astype(F32)
    few = n_causal <= kf
    positive = c0_gt >= kf
    at_zero = jnp.logical_and(jnp.logical_not(positive), c0_ge >= kf)
    hi_pos = smax + jnp.maximum(jnp.abs(smax) * 1e-6, 1e-30)
    lo = jnp.where(few, -F32_MAX, jnp.where(at_zero, 0.0, jnp.where(positive, F32_TINY, smin)))
    c_lo = jnp.where(few, n_causal, jnp.where(at_zero, c0_ge, jnp.where(positive, c0_gt, n_causal)))
    hi = jnp.where(positive, hi_pos, -F32_TINY)
    c_hi = jnp.where(positive, 0.0, c0_ge)
    done = jnp.where(jnp.logical_or(jnp.logical_or(few, at_zero), c_lo <= kf), 1.0, 0.0)

    log_k = float(np.log(kf))
    f_of = lambda cnt: jnp.log(jnp.maximum(cnt, 0.5)) - log_k

    def narrow(carry):
        it, _, lo, hi, c_lo, c_hi, f_lo, f_hi, side, done = carry
        mid = 0.5 * lo + 0.5 * hi
        stuck = jnp.logical_or(mid <= lo, mid >= hi)
        interp = lo + (hi - lo) * jnp.clip(f_lo / jnp.maximum(f_lo - f_hi, 1e-6), 0.01, 0.99)
        cand = jnp.where(jnp.logical_or(interp <= lo, interp >= hi), mid, interp)
        c = count(lambda x: x >= cand)
        f_c = f_of(c)
        active = jnp.logical_and(done < 0.5, jnp.logical_not(stuck))
        up = jnp.logical_and(active, c >= kf)
        down = jnp.logical_and(active, c < kf)
        f_hi = jnp.where(jnp.logical_and(up, side > 0.5), 0.5 * f_hi, f_hi)
        f_lo = jnp.where(jnp.logical_and(down, side < -0.5), 0.5 * f_lo, f_lo)
        lo, c_lo, f_lo = jnp.where(up, cand, lo), jnp.where(up, c, c_lo), jnp.where(up, f_c, f_lo)
        hi, c_hi, f_hi = jnp.where(down, cand, hi), jnp.where(down, c, c_hi), jnp.where(down, f_c, f_hi)
        side = jnp.where(up, 1.0, jnp.where(down, -1.0, side))
        done = jnp.where(jnp.logical_or(stuck, jnp.logical_or(c_lo <= kf, c_lo - c_hi <= 2.0)), 1.0, done)
        return it + 1, jnp.min(done), lo, hi, c_lo, c_hi, f_lo, f_hi, side, done

    carry = (jnp.int32(0), jnp.min(done), lo, hi, c_lo, c_hi, f_of(c_lo), f_of(c_hi),
             jnp.zeros((1, t), F32), done)
    carry = lax.while_loop(lambda c: jnp.logical_and(c[0] < MAX_NARROW_STEPS, c[1] < 0.5), narrow, carry)
    _, all_done, lo, hi, c_lo, _, _, _, _, _ = carry
    thr_ref[0:1, :] = lo
    thr_ref[1:2, :] = c_lo

    extract = jnp.logical_and(c_lo > kf, jnp.logical_not(jnp.logical_or(few, at_zero)))

    @pl.when(jnp.logical_and(all_done > 0.5, jnp.max(jnp.where(extract, 1.0, 0.0)) > 0.5))
    def _():
        def body(kb, best):
            x = sc_ref[kb]
            return jnp.maximum(best, jnp.max(jnp.where(x < hi, x, -jnp.inf), axis=0, keepdims=True))
        below = lax.fori_loop(0, n_kb, body, jnp.full((1, t), -jnp.inf, F32))
        kth = jnp.where(extract, below, lo)
        thr_ref[0:1, :] = kth
        thr_ref[1:2, :] = count(lambda x: x >= kth)

    @pl.when(all_done < 0.5)
    def _():
        def to_float(key):
            return pltpu.bitcast(key ^ ((key >> 31) & 0x7FFFFFFF), F32)

        def search(it, u):
            bit = lax.shift_left(jnp.int32(1), 31 - it)
            cand = to_float((u | bit) ^ INT_MIN)
            return jnp.where(count(lambda x: x >= cand) >= kf, u | bit, u)

        u = lax.fori_loop(0, 32, search, jnp.zeros((1, t), jnp.int32))
        thr_bits = to_float(jnp.maximum(u ^ INT_MIN, KEY_NEG_INF + 1))
        thr_ref[0:1, :] = thr_bits
        thr_ref[1:2, :] = count(lambda x: x >= thr_bits)

    thr = thr_ref[0:1, :]
    has_ties = jnp.max(thr_ref[1:2, :]) > kf

    @pl.when(jnp.logical_not(has_ties))
    def _():
        def body(kb, _):
            sc_ref[kb] = jnp.where(sc_ref[kb] >= thr, 0.0, -jnp.inf)
            return 0
        lax.fori_loop(0, n_kb, body, 0)

    @pl.when(has_ties)
    def _():
        need = kf - count(lambda x: x > thr)
        earlier = jnp.where(qidx < kidx, 1.0, 0.0).astype(BF16)

        def body(kb, run):
            x = sc_ref[kb]
            eq = jnp.where(x == thr, 1.0, 0.0).astype(BF16)
            room = need - run
            rank = jnp.dot(earlier, eq, preferred_element_type=F32)
            tie_mask = jnp.where(rank < room, 0.0, -jnp.inf)
            sc_ref[kb] = jnp.where(x > thr, 0.0, jnp.where(x == thr, tie_mask, -jnp.inf))
            return run + rank[t - 1:t, :] + eq[t - 1:t, :].astype(F32)
        lax.fori_loop(0, n_kb, body, jnp.zeros((1, t), F32))

    add_mask = lambda st, kb: st + sc_ref[kb]
    hg = len(s_refs)

    def head_group(j, _):
        heads = [j * hg + e for e in range(hg)]
        o = _flash_t([qx_ref[0, h] for h in heads], kx_ref, vt_ref, heads, qi, t, add_mask, add_mask, s_refs)
        og_ref[j] = o.astype(og_ref.dtype)
        return 0

    lax.fori_loop(0, n_heads // hg, head_group, 0)
    for j in range(n_heads // hg):
        o_ref[0, :, j * hg * D_HEAD:(j + 1) * hg * D_HEAD] = og_ref[j]


def _dsa(qx, iq, auxt, kx, vt, ik, *, t, k_top, hg):
    bsz, nh, s, _ = qx.shape
    nt = s // t
    return pl.pallas_call(
        functools.partial(_dsa_kernel, t=t, n_heads=nh, k_top=k_top),
        grid=(bsz, nt),
        in_specs=[pl.BlockSpec((1, nh, t, LANES), lambda b, i: (b, 0, i, 0)),
                  pl.BlockSpec((1, t, N_IDX_HEADS * D_IDX), lambda b, i: (b, i, 0)),
                  pl.BlockSpec((1, 1, SUBLANES, t), lambda b, i: (b, i, 0, 0)),
                  pl.BlockSpec((1, nh, s, LANES), lambda b, i: (b, 0, 0, 0), pipeline_mode=pl.Buffered(1)),
                  pl.BlockSpec((1, nh, nt, V_ROWS, t), lambda b, i: (b, 0, 0, 0, 0),
                               pipeline_mode=pl.Buffered(1)),
                  pl.BlockSpec((1, s, LANES), lambda b, i: (b, 0, 0), pipeline_mode=pl.Buffered(1))],
        out_specs=pl.BlockSpec((1, t, nh * D_HEAD), lambda b, i: (b, i, 0)),
        out_shape=jax.ShapeDtypeStruct((bsz, s, nh * D_HEAD), BF16),
        scratch_shapes=[pltpu.VMEM((nt, t, t), F32), pltpu.VMEM((SUBLANES, t), F32),
                        pltpu.VMEM((nh // hg, t, hg * D_HEAD), BF16)]
        + [pltpu.VMEM((t, t), F32)] * hg,
        compiler_params=_cparams(("arbitrary", "arbitrary")),
        name="dsa_attn",
    )(qx, iq, auxt, kx, vt, ik)


def _ffn_kernel(x_ref, yd_ref, yf_ref, dg_ref, xg_ref, g1_ref, sh_ref, sc_ref, g2_ref, ng_ref, fg_ref,
                wo_ref, wg_ref, wu_ref, wd_ref, o_ref, *, hd):
    dot = functools.partial(jnp.dot, preferred_element_type=F32)
    yd = _rms(yd_ref[0].astype(F32), dg_ref[...]).astype(BF16)
    yf = _rms(yf_ref[0].astype(F32), xg_ref[...]).astype(BF16)
    x1 = x_ref[0] + g1_ref[0] * (dot(yd, wo_ref[0:hd, :]) + dot(yf, wo_ref[hd:, :]))
    h = (_rms(x1, ng_ref[...]) * (1.0 + sc_ref[0]) + sh_ref[0]).astype(BF16)
    gate = dot(h, wg_ref[...])
    act = gate * jax.nn.sigmoid(gate) * dot(h, wu_ref[...])
    x2 = x1 + g2_ref[0] * dot(act.astype(BF16), wd_ref[...])
    o_ref[0] = _rms(x2, fg_ref[...])


def _ffn(x, yd, yf, dg, xg, g1, sh, sc, g2, ng, fg, wo, wg, wu, wd, *, tm):
    bsz, s, d = x.shape
    hd = yd.shape[-1]
    tok = lambda width: pl.BlockSpec((1, tm, width), lambda b, i: (b, i, 0))
    per_b = pl.BlockSpec((1, 1, d), lambda b, i: (b, 0, 0))
    const = lambda shape: pl.BlockSpec(shape, lambda b, i: (0, 0), pipeline_mode=pl.Buffered(1))
    return pl.pallas_call(
        functools.partial(_ffn_kernel, hd=hd),
        grid=(bsz, s // tm),
        in_specs=[tok(d), tok(hd), tok(hd), const((1, hd)), const((1, hd)), per_b, per_b, per_b, per_b,
                  const((1, d)), const((1, d)),
                  const(wo.shape), const(wg.shape), const(wu.shape), const(wd.shape)],
        out_specs=tok(d),
        out_shape=jax.ShapeDtypeStruct((bsz, s, d), F32),
        compiler_params=_cparams(("arbitrary", "arbitrary")),
        name="outproj_ffn",
    )(x, yd, yf, dg, xg, g1, sh, sc, g2, ng, fg, wo, wg, wu, wd)


def _pick_tile(n, target):
    t = min(n, target)
    while n % t:
        t //= 2
    return t


def kernel(x, c, positions, ada_w, ada_b, norm_attn_g, w_in, b_forget, dsa_norm_g, fox_norm_g,
           w_out, norm_ffn_g, w_gate, w_up, w_down, final_norm_g):
    bsz, s, d = x.shape
    depth = ada_w.shape[0]
    n_fox = b_forget.shape[1]
    hd = dsa_norm_g.shape[1]
    assert depth == 1, "the fused final RMSNorm assumes a single layer"
    assert n_fox == hd // D_HEAD == 8 and hd % LANES == 0
    k_top = min(TOP_K_MAX, s // 4)
    t_attn = _pick_tile(s, 512)
    tm_in = _pick_tile(s, 1024)
    tm_ffn = _pick_tile(s, 512)

    sizes = (hd, hd, hd, N_IDX_HEADS * D_IDX, D_IDX, N_IDX_HEADS, hd, hd, hd, n_fox)
    pts = np.cumsum((0,) + sizes)
    wl = w_in[0]
    seg = lambda k: wl[:, pts[k]:pts[k + 1]]
    small = jnp.zeros((d, LANES), F32)
    small = small.at[:, 0:N_IDX_HEADS].set(seg(5)).at[:, 8:8 + n_fox].set(seg(9))
    w_cat = jnp.concatenate([seg(0), seg(1), seg(2), seg(6), seg(7), seg(8),
                             seg(3), seg(4), seg(4), small], axis=1).astype(BF16)
    bfv = jnp.zeros((1, LANES), F32).at[0, 8:8 + n_fox].set(b_forget[0])

    lane = jnp.arange(LANES)
    inv_freq = ROPE_THETA ** (-jnp.arange(0, ROT_DIM, 2, dtype=F32) / ROT_DIM)
    invf = jnp.where((lane % D_HEAD) < ROT_DIM, inv_freq[lane % (ROT_DIM // 2)], 0.0).reshape(1, LANES)
    pos = positions.reshape(bsz, s, 1)

    mod = _modulation(c, ada_w[0], ada_b[0])
    sh1, sc1, g1, sh2, sc2, g2 = [m.reshape(bsz, 1, d) for m in jnp.split(mod, N_MOD, axis=-1)]

    qxd, kxd, vtd, qxf, kxf, vtf, iq, ik, auxt = _inproj(
        x, sh1, sc1, norm_attn_g[0].reshape(1, d), pos, invf, w_cat, bfv, hd=hd, tm=tm_in, tk=t_attn)

    yf = _fox(qxf, kxf, vtf, t=t_attn, hg=8)
    yd = _dsa(qxd, iq, auxt, kxd, vtd, ik, t=t_attn, k_top=k_top, hg=8)

    return _ffn(x, yd, yf, dsa_norm_g[0].reshape(1, hd), fox_norm_g[0].reshape(1, hd),
                g1, sh2, sc2, g2, norm_ffn_g[0].reshape(1, d), final_norm_g.reshape(1, d),
                w_out[0].astype(BF16), w_gate[0].astype(BF16), w_up[0].astype(BF16),
                w_down[0].astype(BF16), tm=tm_ffn)
```

```python
import functools

import numpy as np
import jax
import jax.numpy as jnp
from jax import lax
from jax.experimental import pallas as pl
from jax.experimental.pallas import tpu as pltpu

F32 = jnp.float32
BF16 = jnp.bfloat16

D_HEAD = 64
N_IDX_HEADS = 4
D_IDX = 64
TOP_K_MAX = 256
ROPE_THETA = 500000.0
ROT_DIM = D_HEAD // 4
N_MOD = 6
RMS_EPS = 1e-6

LOG2E = 1.4426950408889634
V_ROWS = 80
LANES = 128
SUBLANES = 8
VMEM_LIMIT = 56 * 1024 * 1024

F32_MAX = float(np.finfo(np.float32).max)
F32_TINY = float(np.finfo(np.float32).tiny)
FLASH_LATE_HEADS = 1
MAX_NARROW_STEPS = 48
INT_MIN = -(2 ** 31)
KEY_NEG_INF = int(np.int32(np.uint32(0xFF800000) ^ np.uint32(0x7FFFFFFF)))

NT_DIMS = (((1,), (1,)), ((), ()))


def _cparams(sem):
    return pltpu.CompilerParams(dimension_semantics=sem, vmem_limit_bytes=VMEM_LIMIT)


def _split3(a):
    a0 = a.astype(BF16).astype(F32)
    a1 = (a - a0).astype(BF16).astype(F32)
    return a0, a1, a - a0 - a1


def _rms(x, g):
    return x * lax.rsqrt(jnp.mean(x * x, axis=-1, keepdims=True) + RMS_EPS) * g


def _mod_kernel(c_ref, w_ref, b_ref, o_ref):
    c = c_ref[...]
    a0, a1, _ = _split3(c * jax.nn.sigmoid(c))
    w0, w1, _ = _split3(w_ref[...])
    dot = lambda a, b: jnp.dot(a.astype(BF16), b.astype(BF16), preferred_element_type=F32)
    o_ref[...] = dot(a0, w0) + (dot(a0, w1) + dot(a1, w0)) + b_ref[...]


def _modulation(c, w, b):
    bsz, d = c.shape
    n = w.shape[1]
    tn = 1024
    return pl.pallas_call(
        _mod_kernel,
        grid=(n // tn,),
        in_specs=[pl.BlockSpec((bsz, d), lambda j: (0, 0)),
                  pl.BlockSpec((d, tn), lambda j: (0, j)),
                  pl.BlockSpec((1, tn), lambda j: (0, j))],
        out_specs=pl.BlockSpec((bsz, tn), lambda j: (0, j)),
        out_shape=jax.ShapeDtypeStruct((bsz, n), F32),
        compiler_params=_cparams(("arbitrary",)),
        name="adaln_mod",
    )(c, w, b.reshape(1, n))


def _inproj_kernel(x_ref, sh_ref, sc_ref, g_ref, pos_ref, invf_ref, w_ref, bf_ref,
                   qxd_ref, kxd_ref, vtd_ref, qxf_ref, kxf_ref, vtf_ref, iq_ref, ik_ref, auxt_ref,
                   carry_ref, *, hd, tm, tk, th):
    i = pl.program_id(1)

    @pl.when(i == 0)
    def _():
        carry_ref[...] = jnp.zeros_like(carry_ref)

    lane = lax.broadcasted_iota(jnp.int32, (th, LANES), 1)
    first_half = (lane & (D_HEAD - 1)) < (ROT_DIM // 2)
    in_head = lane < D_HEAD
    one_row = jnp.where(lane == D_HEAD, 1.0, 0.0)
    ones3 = jnp.where(lane < D_HEAD + 3, 1.0, 0.0)
    r = lax.broadcasted_iota(jnp.int32, (th, th), 0)
    c = lax.broadcasted_iota(jnp.int32, (th, th), 1)
    tri = jnp.where(c <= r, 1.0, 0.0).astype(BF16)
    qscale = (D_HEAD ** -0.5) * LOG2E

    def head_of(x, h):
        return pltpu.roll(x, D_HEAD, 1) if h % 2 else x

    for r0 in range(0, tm, th):
        rows = slice(r0, r0 + th)
        u, c0 = r0 // tk, r0 % tk
        h = _rms(x_ref[0, rows], g_ref[...]) * (1.0 + sc_ref[0]) + sh_ref[0]
        proj = jnp.dot(h.astype(BF16), w_ref[...], preferred_element_type=F32)

        ang = pos_ref[0, rows].astype(F32) * invf_ref[...]
        cosv = jnp.cos(ang)
        sinv = jnp.sin(ang)
        sin_signed = jnp.where(first_half, -sinv, sinv)

        def rope(x):
            partner = jnp.where(first_half, pltpu.roll(x, LANES - ROT_DIM // 2, 1),
                                pltpu.roll(x, ROT_DIM // 2, 1))
            return x * cosv + partner * sin_signed

        def slab(base, j):
            return proj[:, base + j * LANES: base + (j + 1) * LANES]

        def put_vt(ref, h, v_slab):
            vt = jnp.where(in_head, head_of(v_slab, h), one_row).T[0:V_ROWS, :]
            ref[0, h, u, :, c0:c0 + th] = vt.astype(BF16)

        for j in range(hd // LANES):
            rq = rope(slab(0, j)) * qscale
            rk = rope(slab(hd, j))
            for h in (2 * j, 2 * j + 1):
                qxd_ref[0, h, rows] = jnp.where(in_head, head_of(rq, h), 0.0).astype(BF16)
                kxd_ref[0, h, rows] = jnp.where(in_head, head_of(rk, h), 0.0).astype(BF16)
                put_vt(vtd_ref, h, slab(2 * hd, j))
        base = 6 * hd
        for j in range(N_IDX_HEADS * D_IDX // LANES):
            iq_ref[0, rows, j * LANES:(j + 1) * LANES] = (rope(slab(base, j)) * (D_IDX ** -0.5)).astype(BF16)
        base += N_IDX_HEADS * D_IDX
        ik_ref[0, rows] = rope(slab(base, 0)).astype(BF16)
        base += LANES

        small = slab(base, 0)
        z = small + bf_ref[...]
        logf = jnp.minimum(z, 0.0) - jnp.log1p(jnp.exp(-jnp.abs(z)))
        logf = jnp.where((lane >= 8) & (lane < 16), logf, 0.0)
        dot = lambda a, b: jnp.dot(a, b.astype(BF16), preferred_element_type=F32)
        l0, l1, l2 = _split3(logf)
        cum = dot(tri, l0) + (dot(tri, l1) + dot(tri, l2)) + carry_ref[...]
        carry_ref[...] = cum[th - 1:th, :]

        aux = jnp.where(lane < N_IDX_HEADS, small * (N_IDX_HEADS ** -0.5), 0.0)
        auxt_ref[0, u, :, c0:c0 + th] = aux.T[0:SUBLANES, :]

        ncum = cum * (-LOG2E)
        for j in range(hd // LANES):
            for h in (2 * j, 2 * j + 1):
                qxf_ref[0, h, rows] = jnp.where(in_head, head_of(slab(3 * hd, j), h) * qscale, ones3).astype(BF16)
                p0, p1, p2 = _split3(ncum[:, 8 + h:9 + h])
                extra = jnp.where(lane == D_HEAD, p0,
                                  jnp.where(lane == D_HEAD + 1, p1, jnp.where(lane == D_HEAD + 2, p2, 0.0)))
                kxf_ref[0, h, rows] = jnp.where(in_head, head_of(slab(4 * hd, j), h), extra).astype(BF16)
                put_vt(vtf_ref, h, slab(5 * hd, j))


def _inproj(x, sh, sc, g, pos, invf, w, bfv, *, hd, tm, tk):
    bsz, s, d = x.shape
    n_cols = w.shape[1]
    nh = hd // D_HEAD
    tok = lambda width: pl.BlockSpec((1, tm, width), lambda b, i: (b, i, 0))
    per_b = pl.BlockSpec((1, 1, d), lambda b, i: (b, 0, 0))
    const = lambda shape: pl.BlockSpec(shape, lambda b, i: (0,) * len(shape), pipeline_mode=pl.Buffered(1))
    headed = jax.ShapeDtypeStruct((bsz, nh, s, LANES), BF16)
    headed_spec = pl.BlockSpec((1, nh, tm, LANES), lambda b, i: (b, 0, i, 0))
    vt = jax.ShapeDtypeStruct((bsz, nh, s // tk, V_ROWS, tk), BF16)
    vt_spec = pl.BlockSpec((1, nh, tm // tk, V_ROWS, tk), lambda b, i: (b, 0, i, 0, 0))
    out_shape = [headed, headed, vt, headed, headed, vt,
                 jax.ShapeDtypeStruct((bsz, s, N_IDX_HEADS * D_IDX), BF16),
                 jax.ShapeDtypeStruct((bsz, s, LANES), BF16),
                 jax.ShapeDtypeStruct((bsz, s // tk, SUBLANES, tk), F32)]
    out_specs = [headed_spec, headed_spec, vt_spec, headed_spec, headed_spec, vt_spec,
                 tok(N_IDX_HEADS * D_IDX), tok(LANES),
                 pl.BlockSpec((1, tm // tk, SUBLANES, tk), lambda b, i: (b, i, 0, 0))]
    return pl.pallas_call(
        functools.partial(_inproj_kernel, hd=hd, tm=tm, tk=tk, th=min(tm, 256)),
        grid=(bsz, s // tm),
        in_specs=[tok(d), per_b, per_b, const((1, d)), tok(1), const((1, LANES)),
                  const((d, n_cols)), const((1, LANES))],
        out_specs=out_specs,
        out_shape=out_shape,
        scratch_shapes=[pltpu.VMEM((1, LANES), F32)],
        compiler_params=_cparams(("arbitrary", "arbitrary")),
        name="inproj",
    )(x, sh, sc, g, pos, invf, w, bfv)


def _flash_t(q, kx_ref, vt_ref, heads, n_plain, t, plain_fn, last_fn, s_refs):
    nh = len(heads)

    def produce(e, kb, fn):
        off = pl.multiple_of(kb * t, t)
        st = fn(lax.dot_general(kx_ref[0, heads[e], pl.ds(off, t), :], q[e], NT_DIMS,
                                preferred_element_type=F32), kb)
        s_refs[e][...] = st
        return jnp.max(st, axis=0, keepdims=True)

    def consume(e, kb, state, bmax):
        m, acc = state
        m_new = jnp.maximum(m, bmax)
        pt = jnp.exp2(s_refs[e][...] - m_new).astype(BF16)
        acc = jnp.exp2(m - m_new) * acc + jnp.dot(vt_ref[0, heads[e], kb], pt, preferred_element_type=F32)
        return m_new, acc

    n_late = min(FLASH_LATE_HEADS, nh - 1)
    early, late = range(nh - n_late), range(nh - n_late, nh)

    def body(kb, carry):
        states, bmaxes = carry
        states = list(states)
        cur = jnp.where(kb == 0, n_plain, kb - 1)
        late_bmax = [produce(e, kb, plain_fn) for e in late]
        new_bmaxes = []
        for e in early:
            states[e] = consume(e, cur, states[e], bmaxes[e])
            new_bmaxes.append(produce(e, kb, plain_fn))
        for e, bm in zip(late, late_bmax):
            states[e] = consume(e, kb, states[e], bm)
        return tuple(states), tuple(new_bmaxes)

    states = tuple((jnp.full((1, t), -1e30, F32), jnp.zeros((V_ROWS, t), F32)) for _ in heads)
    bmaxes = tuple(produce(e, n_plain, last_fn) for e in early)
    states, bmaxes = lax.fori_loop(0, n_plain, body, (states, bmaxes))
    states = list(states)
    late_bmax = [produce(e, n_plain, last_fn) for e in late]
    cur = jnp.maximum(n_plain - 1, 0)
    for e in early:
        states[e] = consume(e, cur, states[e], bmaxes[e])
    for e, bm in zip(late, late_bmax):
        states[e] = consume(e, n_plain, states[e], bm)
    o = jnp.concatenate([acc[0:D_HEAD] / acc[D_HEAD:D_HEAD + 1] for _, acc in states], axis=0)
    return o.T


def _fox_kernel(qx_ref, kx_ref, vt_ref, o_ref, *s_refs, t, hg):
    qi = pl.program_id(2)
    causal = lax.broadcasted_iota(jnp.int32, (t, t), 0) <= lax.broadcasted_iota(jnp.int32, (t, t), 1)
    o = _flash_t([qx_ref[0, e] for e in range(hg)], kx_ref, vt_ref, tuple(range(hg)), qi, t,
                 lambda st, kb: st, lambda st, kb: jnp.where(causal, st, -jnp.inf), s_refs)
    o_ref[0] = o.astype(o_ref.dtype)


def _fox(qx, kx, vt, *, t, hg):
    bsz, nh, s, _ = qx.shape
    nt = s // t
    return pl.pallas_call(
        functools.partial(_fox_kernel, t=t, hg=hg),
        grid=(bsz, nh // hg, nt),
        in_specs=[pl.BlockSpec((1, hg, t, LANES), lambda b, j, i: (b, j, i, 0)),
                  pl.BlockSpec((1, hg, s, LANES), lambda b, j, i: (b, j, 0, 0)),
                  pl.BlockSpec((1, hg, nt, V_ROWS, t), lambda b, j, i: (b, j, 0, 0, 0))],
        out_specs=pl.BlockSpec((1, t, hg * D_HEAD), lambda b, j, i: (b, i, j)),
        out_shape=jax.ShapeDtypeStruct((bsz, s, nh * D_HEAD), BF16),
        scratch_shapes=[pltpu.VMEM((t, t), F32)] * hg,
        compiler_params=_cparams(("arbitrary", "arbitrary", "arbitrary")),
        name="fox_attn",
    )(qx, kx, vt)


def _dsa_kernel(qx_ref, iq_ref, auxt_ref, kx_ref, vt_ref, ik_ref, o_ref, sc_ref, thr_ref, og_ref, *s_refs,
                t, n_heads, k_top):
    qi = pl.program_id(1)
    n_kb = qi + 1
    kidx = lax.broadcasted_iota(jnp.int32, (t, t), 0)
    qidx = lax.broadcasted_iota(jnp.int32, (t, t), 1)
    lane = lax.broadcasted_iota(jnp.int32, (t, LANES), 1)
    iw = auxt_ref[0, 0]

    iqm = []
    for h in range(N_IDX_HEADS):
        iq2 = iq_ref[0, :, (h // 2) * LANES:(h // 2 + 1) * LANES]
        keep = (lane < D_IDX) if h % 2 == 0 else (lane >= D_IDX)
        iqm.append(iq2 * jnp.where(keep, 1.0, 0.0).astype(BF16))

    def score_block(kb, masked, stats):
        off = pl.multiple_of(kb * t, t)
        ik2 = ik_ref[0, pl.ds(off, t), :]
        sc = jnp.zeros((t, t), F32)
        for h in range(N_IDX_HEADS):
            d = lax.dot_general(ik2, iqm[h], NT_DIMS, preferred_element_type=F32)
            sc = sc + jnp.maximum(d, 0.0) * iw[h:h + 1, :]
        sc_min = sc
        if masked:
            causal = kidx <= qidx
            sc_min = jnp.where(causal, sc, jnp.inf)
            sc = jnp.where(causal, sc, -jnp.inf)
        sc_ref[kb] = sc
        col = lambda x, op: op(x, axis=0, keepdims=True)
        return (jnp.minimum(stats[0], col(sc_min, jnp.min)), jnp.maximum(stats[1], col(sc, jnp.max)),
                stats[2] + col(jnp.where(sc >= 0.0, 1.0, 0.0), jnp.sum),
                stats[3] + col(jnp.where(sc > 0.0, 1.0, 0.0), jnp.sum))

    zeros = jnp.zeros((1, t), F32)
    stats = (jnp.full((1, t), jnp.inf, F32), jnp.full((1, t), -jnp.inf, F32), zeros, zeros)
    stats = lax.fori_loop(0, qi, lambda kb, c: score_block(kb, False, c), stats)
    smin, smax, c0_ge, c0_gt = score_block(qi, True, stats)

    def count(cmp_fn):
        def body(kb, tot):
            hit = jnp.where(cmp_fn(sc_ref[kb]), 1.0, 0.0)
            return tot + jnp.sum(hit.reshape(t // SUBLANES, SUBLANES, t), axis=0)
        tot = lax.fori_loop(0, n_kb, body, jnp.zeros((SUBLANES, t), F32))
        return jnp.sum(tot, axis=0, keepdims=True)

    kf = float(k_top)
    n_causal = (qi * t + 1 + lax.broadcasted_iota(jnp.int32, (1, t), 1)).astype(F32)
    few = n_causal <= kf
    positive = c0_gt >= kf
    at_zero = jnp.logical_and(jnp.logical_not(positive), c0_ge >= kf)
    hi_pos = smax + jnp.maximum(jnp.abs(smax) * 1e-6, 1e-30)
    lo = jnp.where(few, -F32_MAX, jnp.where(at_zero, 0.0, jnp.where(positive, F32_TINY, smin)))
    c_lo = jnp.where(few, n_causal, jnp.where(at_zero, c0_ge, jnp.where(positive, c0_gt, n_causal)))
    hi = jnp.where(positive, hi_pos, -F32_TINY)
    c_hi = jnp.where(positive, 0.0, c0_ge)
    done = jnp.where(jnp.logical_or(jnp.logical_or(few, at_zero), c_lo <= kf), 1.0, 0.0)

    log_k = float(np.log(kf))
    f_of = lambda cnt: jnp.log(jnp.maximum(cnt, 0.5)) - log_k

    def narrow(carry):
        it, _, lo, hi, c_lo, c_hi, f_lo, f_hi, side, done = carry
        mid = 0.5 * lo + 0.5 * hi
        stuck = jnp.logical_or(mid <= lo, mid >= hi)
        interp = lo + (hi - lo) * jnp.clip(f_lo / jnp.maximum(f_lo - f_hi, 1e-6), 0.01, 0.99)
        cand = jnp.where(jnp.logical_or(interp <= lo, interp >= hi), mid, interp)
        c = count(lambda x: x >= cand)
        f_c = f_of(c)
        active = jnp.logical_and(done < 0.5, jnp.logical_not(stuck))
        up = jnp.logical_and(active, c >= kf)
        down = jnp.logical_and(active, c < kf)
        f_hi = jnp.where(jnp.logical_and(up, side > 0.5), 0.5 * f_hi, f_hi)
        f_lo = jnp.where(jnp.logical_and(down, side < -0.5), 0.5 * f_lo, f_lo)
        lo, c_lo, f_lo = jnp.where(up, cand, lo), jnp.where(up, c, c_lo), jnp.where(up, f_c, f_lo)
        hi, c_hi, f_hi = jnp.where(down, cand, hi), jnp.where(down, c, c_hi), jnp.where(down, f_c, f_hi)
        side = jnp.where(up, 1.0, jnp.where(down, -1.0, side))
        done = jnp.where(jnp.logical_or(stuck, jnp.logical_or(c_lo <= kf, c_lo - c_hi <= 2.0)), 1.0, done)
        return it + 1, jnp.min(done), lo, hi, c_lo, c_hi, f_lo, f_hi, side, done

    carry = (jnp.int32(0), jnp.min(done), lo, hi, c_lo, c_hi, f_of(c_lo), f_of(c_hi),
             jnp.zeros((1, t), F32), done)
    carry = lax.while_loop(lambda c: jnp.logical_and(c[0] < MAX_NARROW_STEPS, c[1] < 0.5), narrow, carry)
    _, all_done, lo, hi, c_lo, _, _, _, _, _ = carry
    thr_ref[0:1, :] = lo
    thr_ref[1:2, :] = c_lo

    extract = jnp.logical_and(c_lo > kf, jnp.logical_not(jnp.logical_or(few, at_zero)))

    @pl.when(jnp.logical_and(all_done > 0.5, jnp.max(jnp.where(extract, 1.0, 0.0)) > 0.5))
    def _():
        def body(kb, best):
            x = sc_ref[kb]
            return jnp.maximum(best, jnp.max(jnp.where(x < hi, x, -jnp.inf), axis=0, keepdims=True))
        below = lax.fori_loop(0, n_kb, body, jnp.full((1, t), -jnp.inf, F32))
        kth = jnp.where(extract, below, lo)
        thr_ref[0:1, :] = kth
        thr_ref[1:2, :] = count(lambda x: x >= kth)

    @pl.when(all_done < 0.5)
    def _():
        def to_float(key):
            return pltpu.bitcast(key ^ ((key >> 31) & 0x7FFFFFFF), F32)

        def search(it, u):
            bit = lax.shift_left(jnp.int32(1), 31 - it)
            cand = to_float((u | bit) ^ INT_MIN)
            return jnp.where(count(lambda x: x >= cand) >= kf, u | bit, u)

        u = lax.fori_loop(0, 32, search, jnp.zeros((1, t), jnp.int32))
        thr_bits = to_float(jnp.maximum(u ^ INT_MIN, KEY_NEG_INF + 1))
        thr_ref[0:1, :] = thr_bits
        thr_ref[1:2, :] = count(lambda x: x >= thr_bits)

    thr = thr_ref[0:1, :]
    has_ties = jnp.max(thr_ref[1:2, :]) > kf

    @pl.when(jnp.logical_not(has_ties))
    def _():
        def body(kb, _):
            sc_ref[kb] = jnp.where(sc_ref[kb] >= thr, 0.0, -jnp.inf)
            return 0
        lax.fori_loop(0, n_kb, body, 0)

    @pl.when(has_ties)
    def _():
        need = kf - count(lambda x: x > thr)
        earlier = jnp.where(qidx < kidx, 1.0, 0.0).astype(BF16)

        def body(kb, run):
            x = sc_ref[kb]
            eq = jnp.where(x == thr, 1.0, 0.0).astype(BF16)
            room = need - run
            rank = jnp.dot(earlier, eq, preferred_element_type=F32)
            tie_mask = jnp.where(rank < room, 0.0, -jnp.inf)
            sc_ref[kb] = jnp.where(x > thr, 0.0, jnp.where(x == thr, tie_mask, -jnp.inf))
            return run + rank[t - 1:t, :] + eq[t - 1:t, :].astype(F32)
        lax.fori_loop(0, n_kb, body, jnp.zeros((1, t), F32))

    add_mask = lambda st, kb: st + sc_ref[kb]
    hg = len(s_refs)

    def head_group(j, _):
        heads = [j * hg + e for e in range(hg)]
        o = _flash_t([qx_ref[0, h] for h in heads], kx_ref, vt_ref, heads, qi, t, add_mask, add_mask, s_refs)
        og_ref[j] = o.astype(og_ref.dtype)
        return 0

    lax.fori_loop(0, n_heads // hg, head_group, 0)
    for j in range(n_heads // hg):
        o_ref[0, :, j * hg * D_HEAD:(j + 1) * hg * D_HEAD] = og_ref[j]


def _dsa(qx, iq, auxt, kx, vt, ik, *, t, k_top, hg):
    bsz, nh, s, _ = qx.shape
    nt = s // t
    return pl.pallas_call(
        functools.partial(_dsa_kernel, t=t, n_heads=nh, k_top=k_top),
        grid=(bsz, nt),
        in_specs=[pl.BlockSpec((1, nh, t, LANES), lambda b, i: (b, 0, i, 0)),
                  pl.BlockSpec((1, t, N_IDX_HEADS * D_IDX), lambda b, i: (b, i, 0)),
                  pl.BlockSpec((1, 1, SUBLANES, t), lambda b, i: (b, i, 0, 0)),
                  pl.BlockSpec((1, nh, s, LANES), lambda b, i: (b, 0, 0, 0), pipeline_mode=pl.Buffered(1)),
                  pl.BlockSpec((1, nh, nt, V_ROWS, t), lambda b, i: (b, 0, 0, 0, 0),
                               pipeline_mode=pl.Buffered(1)),
                  pl.BlockSpec((1, s, LANES), lambda b, i: (b, 0, 0), pipeline_mode=pl.Buffered(1))],
        out_specs=pl.BlockSpec((1, t, nh * D_HEAD), lambda b, i: (b, i, 0)),
        out_shape=jax.ShapeDtypeStruct((bsz, s, nh * D_HEAD), BF16),
        scratch_shapes=[pltpu.VMEM((nt, t, t), F32), pltpu.VMEM((SUBLANES, t), F32),
                        pltpu.VMEM((nh // hg, t, hg * D_HEAD), BF16)]
        + [pltpu.VMEM((t, t), F32)] * hg,
        compiler_params=_cparams(("arbitrary", "arbitrary")),
        name="dsa_attn",
    )(qx, iq, auxt, kx, vt, ik)


def _ffn_kernel(x_ref, yd_ref, yf_ref, dg_ref, xg_ref, g1_ref, sh_ref, sc_ref, g2_ref, ng_ref, fg_ref,
                wo_ref, wg_ref, wu_ref, wd_ref, o_ref, *, hd):
    dot = functools.partial(jnp.dot, preferred_element_type=F32)
    yd = _rms(yd_ref[0].astype(F32), dg_ref[...]).astype(BF16)
    yf = _rms(yf_ref[0].astype(F32), xg_ref[...]).astype(BF16)
    x1 = x_ref[0] + g1_ref[0] * (dot(yd, wo_ref[0:hd, :]) + dot(yf, wo_ref[hd:, :]))
    h = (_rms(x1, ng_ref[...]) * (1.0 + sc_ref[0]) + sh_ref[0]).astype(BF16)
    gate = dot(h, wg_ref[...])
    act = gate * jax.nn.sigmoid(gate) * dot(h, wu_ref[...])
    x2 = x1 + g2_ref[0] * dot(act.astype(BF16), wd_ref[...])
    o_ref[0] = _rms(x2, fg_ref[...])


def _ffn(x, yd, yf, dg, xg, g1, sh, sc, g2, ng, fg, wo, wg, wu, wd, *, tm):
    bsz, s, d = x.shape
    hd = yd.shape[-1]
    tok = lambda width: pl.BlockSpec((1, tm, width), lambda b, i: (b, i, 0))
    per_b = pl.BlockSpec((1, 1, d), lambda b, i: (b, 0, 0))
    const = lambda shape: pl.BlockSpec(shape, lambda b, i: (0, 0), pipeline_mode=pl.Buffered(1))
    return pl.pallas_call(
        functools.partial(_ffn_kernel, hd=hd),
        grid=(bsz, s // tm),
        in_specs=[tok(d), tok(hd), tok(hd), const((1, hd)), const((1, hd)), per_b, per_b, per_b, per_b,
                  const((1, d)), const((1, d)),
                  const(wo.shape), const(wg.shape), const(wu.shape), const(wd.shape)],
        out_specs=tok(d),
        out_shape=jax.ShapeDtypeStruct((bsz, s, d), F32),
        compiler_params=_cparams(("arbitrary", "arbitrary")),
        name="outproj_ffn",
    )(x, yd, yf, dg, xg, g1, sh, sc, g2, ng, fg, wo, wg, wu, wd)


def _pick_tile(n, target):
    t = min(n, target)
    while n % t:
        t //= 2
    return t


def kernel(x, c, positions, ada_w, ada_b, norm_attn_g, w_in, b_forget, dsa_norm_g, fox_norm_g,
           w_out, norm_ffn_g, w_gate, w_up, w_down, final_norm_g):
    bsz, s, d = x.shape
    depth = ada_w.shape[0]
    n_fox = b_forget.shape[1]
    hd = dsa_norm_g.shape[1]
    assert depth == 1, "the fused final RMSNorm assumes a single layer"
    assert n_fox == hd // D_HEAD == 8 and hd % LANES == 0
    k_top = min(TOP_K_MAX, s // 4)
    t_attn = _pick_tile(s, 512)
    tm_in = _pick_tile(s, 1024)
    tm_ffn = _pick_tile(s, 512)

    sizes = (hd, hd, hd, N_IDX_HEADS * D_IDX, D_IDX, N_IDX_HEADS, hd, hd, hd, n_fox)
    pts = np.cumsum((0,) + sizes)
    wl = w_in[0]
    seg = lambda k: wl[:, pts[k]:pts[k + 1]]
    small = jnp.zeros((d, LANES), F32)
    small = small.at[:, 0:N_IDX_HEADS].set(seg(5)).at[:, 8:8 + n_fox].set(seg(9))
    w_cat = jnp.concatenate([seg(0), seg(1), seg(2), seg(6), seg(7), seg(8),
                             seg(3), seg(4), seg(4), small], axis=1).astype(BF16)
    bfv = jnp.zeros((1, LANES), F32).at[0, 8:8 + n_fox].set(b_forget[0])

    lane = jnp.arange(LANES)
    inv_freq = ROPE_THETA ** (-jnp.arange(0, ROT_DIM, 2, dtype=F32) / ROT_DIM)
    invf = jnp.where((lane % D_HEAD) < ROT_DIM, inv_freq[lane % (ROT_DIM // 2)], 0.0).reshape(1, LANES)
    pos = positions.reshape(bsz, s, 1)

    mod = _modulation(c, ada_w[0], ada_b[0])
    sh1, sc1, g1, sh2, sc2, g2 = [m.reshape(bsz, 1, d) for m in jnp.split(mod, N_MOD, axis=-1)]

    qxd, kxd, vtd, qxf, kxf, vtf, iq, ik, auxt = _inproj(
        x, sh1, sc1, norm_attn_g[0].reshape(1, d), pos, invf, w_cat, bfv, hd=hd, tm=tm_in, tk=t_attn)

    yf = _fox(qxf, kxf, vtf, t=t_attn, hg=8)
    yd = _dsa(qxd, iq, auxt, kxd, vtd, ik, t=t_attn, k_top=k_top, hg=8)

    return _ffn(x, yd, yf, dsa_norm_g[0].reshape(1, hd), fox_norm_g[0].reshape(1, hd),
                g1, sh2, sc2, g2, norm_ffn_g[0].reshape(1, d), final_norm_g.reshape(1, d),
                w_out[0].astype(BF16), w_gate[0].astype(BF16), w_up[0].astype(BF16),
                w_down[0].astype(BF16), tm=tm_ffn)
```
